```python
import math
import jax
import jax.numpy as jnp
from jax import lax
import numpy as np

D_MODEL = 1024
BATCH = 32
SEQ = 256
DEPTH = 4
DEC_BATCH = 4
DEC_SEQ = 1024
PAST_LEN = 256

GRID_W = 64
N_MIXERS = 3
N_LAYERS_A = (DEPTH + 2) // 3
N_LAYERS_B = (DEPTH + 1) // 3
N_LAYERS_C = DEPTH // 3
N_MOD = 6
D_FF = 4 * D_MODEL
DEEPNORM_ALPHA = (2.0 * DEPTH) ** 0.25
DEEPNORM_BETA = (8.0 * DEPTH) ** -0.25
LN_EPS = 1e-5

RW_HEAD = 64
RW_HEADS = D_MODEL // RW_HEAD
RW_DECAY_RANK = 64
RW_ICL_RANK = 64
RW_GATE_RANK = 128
RW_GN_EPS = 64e-5

AT_HEAD = 64
AT_HEADS = D_MODEL // AT_HEAD
AT_KV_HEADS = AT_HEADS // 4
Q_BLOCK = 128
ROPE_THETA = 10000.0

DN_HEADS = 8
DN_HEAD_K = D_MODEL // DN_HEADS
DN_HEAD_V = D_MODEL // DN_HEADS
DN_CONV = 3
DN_CHUNK = 64

kernel_name = 'hybrid_dit_rwkv7_gqa_gdn_step'


def layer_norm(x, g, b):
    xf = x.astype(jnp.float32)
    mu = jnp.mean(xf, axis=-1, keepdims=True)
    var = jnp.mean(jnp.square(xf - mu), axis=-1, keepdims=True)
    return ((xf - mu) * lax.rsqrt(var + LN_EPS)).astype(x.dtype) * g + b


def rms_norm(x, g, eps=1e-6):
    xf = x.astype(jnp.float32)
    y = xf * lax.rsqrt(jnp.mean(jnp.square(xf), axis=-1, keepdims=True) + eps)
    return y.astype(x.dtype) * g


def l2_normalize(x, eps=1e-6):
    xf = x.astype(jnp.float32)
    return (xf * lax.rsqrt(jnp.sum(jnp.square(xf), axis=-1, keepdims=True) + eps)).astype(x.dtype)


def modulation(cond, w_mod, b_mod):
    m = jax.nn.silu(cond) @ w_mod + b_mod
    return jnp.split(m[..., None, :], N_MOD, axis=-1)


def modulate(x, shift, scale):
    return x * (1 + scale) + shift


def deepnorm_residual(x, delta, gate, g, b):
    return layer_norm(DEEPNORM_ALPHA * x + gate * delta, g, b)


def squared_relu_mlp(h, w_fc1, w_fc2):
    return jnp.square(jax.nn.relu(h @ w_fc1)) @ w_fc2


def axial_rope_tables(n_tokens, head_dim):
    rows = n_tokens // GRID_W
    row = jnp.repeat(jnp.arange(rows), GRID_W)
    col = jnp.tile(jnp.arange(GRID_W), rows)
    n_freq = head_dim // 4
    inv_freq = ROPE_THETA ** (-jnp.arange(n_freq, dtype=jnp.float32) / n_freq)
    ang = jnp.stack([row, col], axis=-1).astype(jnp.float32)[:, :, None] * inv_freq
    return jnp.cos(ang), jnp.sin(ang)


def apply_axial_rope(x, cos, sin):
    b, t, h, hd = x.shape
    n_freq = hd // 4
    xr = x.astype(jnp.float32).reshape(b, t, h, 2, 2, n_freq)
    x1, x2 = xr[..., 0, :], xr[..., 1, :]
    cos, sin = cos[None, :, None], sin[None, :, None]
    out = jnp.stack([x1 * cos - x2 * sin, x2 * cos + x1 * sin], axis=-2)
    return out.reshape(b, t, h, hd).astype(x.dtype)


def attention_qkv(h, w_qkv, q_gain, k_gain):
    b, t, _ = h.shape
    qkv = h @ w_qkv
    q, k, v = jnp.split(qkv, [AT_HEADS * AT_HEAD, (AT_HEADS + AT_KV_HEADS) * AT_HEAD], axis=-1)
    q = rms_norm(q.reshape(b, t, AT_HEADS, AT_HEAD), q_gain)
    k = rms_norm(k.reshape(b, t, AT_KV_HEADS, AT_HEAD), k_gain)
    v = v.reshape(b, t, AT_KV_HEADS, AT_HEAD)
    return q, k, v


def block_attention(q, k, v):
    b, tq, h, hd = q.shape
    grp = h // AT_KV_HEADS
    nb = tq // Q_BLOCK
    scale = hd ** -0.5
    qb = q.reshape(b, nb, Q_BLOCK, AT_KV_HEADS, grp, hd).transpose(1, 0, 2, 3, 4, 5)

    def one_block(q_blk):
        s = jnp.einsum('bqhgd,bkhd->bhgqk', q_blk, k).astype(jnp.float32) * scale
        p = jax.nn.softmax(s, axis=-1).astype(v.dtype)
        return jnp.einsum('bhgqk,bkhd->bqhgd', p, v)

    o = lax.map(one_block, qb)
    return o.transpose(1, 0, 2, 3, 4, 5).reshape(b, tq, h * hd)


def attention_context(h, w_qkv, q_gain, k_gain, w_o):
    q, k, v = attention_qkv(h, w_qkv, q_gain, k_gain)
    return block_attention(q, k, v) @ w_o, k, v


def attention_latent(h, ctx_k, ctx_v, w_qkv, q_gain, k_gain, w_o):
    q, k, v = attention_qkv(h, w_qkv, q_gain, k_gain)
    cos, sin = axial_rope_tables(h.shape[1], AT_HEAD)
    q = apply_axial_rope(q, cos, sin)
    k = apply_axial_rope(k, cos, sin)
    k_all = jnp.concatenate([ctx_k.astype(k.dtype), k], axis=1)
    v_all = jnp.concatenate([ctx_v.astype(v.dtype), v], axis=1)
    return block_attention(q, k_all, v_all) @ w_o


def centred_token_shift(x):
    prev = jnp.pad(x[:, :-1], ((0, 0), (1, 0), (0, 0)))
    nxt = jnp.pad(x[:, 1:], ((0, 0), (0, 1), (0, 0)))
    return 0.5 * (prev + nxt) - x


def rwkv7_scan(r, w, k, v, kk, a, s0):
    def step(s, inp):
        r_t, w_t, k_t, v_t, kk_t, a_t = inp
        sa = jnp.einsum('bhvk,bhk->bhv', s, -kk_t)
        s = (s * w_t[:, :, None, :] + sa[..., None] * (kk_t * a_t)[:, :, None, :]
             + v_t[..., None] * k_t[:, :, None, :])
        return s, jnp.einsum('bhvk,bhk->bhv', s, r_t)

    xs = tuple(jnp.swapaxes(z, 0, 1) for z in (r, w, k, v, kk, a))
    s_final, o = lax.scan(step, s0.astype(jnp.float32), xs)
    return jnp.swapaxes(o, 0, 1), s_final


def rwkv7_mixer(h, s0_fwd, s0_bwd, mu_mix, w_rkv, w0, w1, w2, a0, a1, a2, g1, g2,
                k_k, k_a, r_k, gn_g, gn_b, w_o):
    b, t, d = h.shape

    def heads(z):
        return z.astype(jnp.float32).reshape(z.shape[:-1] + (RW_HEADS, RW_HEAD))

    def flip(z):
        return jnp.flip(z, axis=1)

    xx = centred_token_shift(h)
    xr, xw, xk, xv, xa, xg = [h + xx * mu_mix[i] for i in range(6)]
    r = xr @ w_rkv[0]
    k = xk @ w_rkv[1]
    v = xv @ w_rkv[2]
    lw = jnp.tanh(jnp.einsum('btc,ecr->betr', xw, w1))
    w_log = -jax.nn.softplus(-(w0[None, :, None, :] + jnp.einsum('betr,erc->betc', lw, w2))) - 0.5
    decay = jnp.exp(-jnp.exp(w_log.astype(jnp.float32)))
    la = jnp.einsum('btc,ecr->betr', xa, a1)
    a = jax.nn.sigmoid(a0[None, :, None, :] + jnp.einsum('betr,erc->betc', la, a2))
    g = jax.nn.sigmoid(xg @ g1) @ g2
    kk = l2_normalize(heads(k * k_k))
    k_dir = heads(k[:, None] * (1 + (a - 1) * k_a))
    a_h, decay_h = heads(a), heads(decay)
    rh, vh = heads(r), heads(v)
    o_f, s_f = rwkv7_scan(rh, decay_h[:, 0], k_dir[:, 0], vh, kk, a_h[:, 0], s0_fwd)
    o_b, s_b = rwkv7_scan(flip(rh), flip(decay_h[:, 1]), flip(k_dir[:, 1]), flip(vh),
                          flip(kk), flip(a_h[:, 1]), s0_bwd)
    o = o_f + flip(o_b)
    mean = jnp.mean(o, axis=-1, keepdims=True)
    var = jnp.mean(jnp.square(o - mean), axis=-1, keepdims=True)
    y = ((o - mean) * lax.rsqrt(var + RW_GN_EPS)).reshape(b, t, d) * gn_g + gn_b
    bonus = jnp.sum(rh[:, None] * k_dir * r_k, axis=-1).sum(axis=1)[..., None] * vh
    y = (y + bonus.reshape(b, t, d)) * g
    return y.astype(h.dtype) @ w_o, s_f, s_b


def short_conv(x, w):
    width, ch = w.shape
    return lax.conv_general_dilated(
        x, w[:, None, :].astype(x.dtype), window_strides=(1,),
        padding=[((width - 1) // 2, width // 2)],
        dimension_numbers=('NWC', 'WIO', 'NWC'), feature_group_count=ch)


def gated_delta_chunked(q, k, v, g, beta, s0):
    b, t, h, _ = q.shape
    dv = v.shape[-1]
    n = t // DN_CHUNK

    def chunks(z):
        z = z.astype(jnp.float32).reshape((b, n, DN_CHUNK) + z.shape[2:])
        return jnp.moveaxis(z, 3, 1)

    q, k, v, g, beta = (chunks(z) for z in (q, k, v, g, beta))
    g = jnp.cumsum(g, axis=-1)
    kb = k * beta[..., None]
    vb = v * beta[..., None]
    idx = jnp.arange(DN_CHUNK)
    incl = idx[:, None] >= idx[None, :]
    strict = idx[:, None] > idx[None, :]
    diff = g[..., :, None] - g[..., None, :]
    decay = jnp.where(incl, jnp.exp(jnp.where(incl, diff, 0.0)), 0.0)
    low = jnp.where(strict, jnp.einsum('bhncd,bhnsd->bhncs', kb, k) * decay, 0.0)
    eye = jnp.eye(DN_CHUNK, dtype=jnp.float32)
    tmat = lax.linalg.triangular_solve(eye + low, jnp.broadcast_to(eye, low.shape),
                                       left_side=True, lower=True)
    u = tmat @ vb
    w = tmat @ (kb * jnp.exp(g)[..., None])
    qk = jnp.einsum('bhncd,bhnsd->bhncs', q, k) * decay
    q_dec = q * jnp.exp(g)[..., None]
    k_dec = k * jnp.exp(g[..., -1:] - g)[..., None]
    g_last = jnp.exp(g[..., -1])

    def step(s, inp):
        q_i, k_i, u_i, w_i, qk_i, gl_i = inp
        v_new = u_i - w_i @ s
        o_i = q_i @ s + qk_i @ v_new
        s = s * gl_i[..., None, None] + jnp.einsum('bhcd,bhce->bhde', k_i, v_new)
        return s, o_i

    xs = tuple(jnp.moveaxis(z, 2, 0) for z in (q_dec, k_dec, u, w, qk, g_last))
    s_final, o = lax.scan(step, s0.astype(jnp.float32), xs)
    o = jnp.moveaxis(jnp.moveaxis(o, 0, 2), 1, 3).reshape(b, t, h, dv)
    return o, s_final


def deltanet_mixer(h, s0_fwd, s0_bwd, w_in, conv_w, a_log, dt_bias, norm_g, w_o):
    b, t, _ = h.shape
    hk, hv = DN_HEADS * DN_HEAD_K, DN_HEADS * DN_HEAD_V

    def flip(z):
        return jnp.flip(z, axis=1)

    proj = h @ w_in
    qkv, z, gate_a, gate_b = jnp.split(
        proj, [2 * hk + hv, 2 * hk + 2 * hv, 2 * hk + 2 * hv + 2 * DN_HEADS], axis=-1)
    qkv = jax.nn.silu(short_conv(qkv, conv_w))
    q, k, v = jnp.split(qkv, [hk, 2 * hk], axis=-1)
    q = l2_normalize(q.reshape(b, t, DN_HEADS, DN_HEAD_K)) * (DN_HEAD_K ** -0.5)
    k = l2_normalize(k.reshape(b, t, DN_HEADS, DN_HEAD_K))
    v = v.reshape(b, t, DN_HEADS, DN_HEAD_V)
    gate_a = gate_a.astype(jnp.float32).reshape(b, t, 2, DN_HEADS)
    log_decay = -jnp.exp(a_log) * jax.nn.softplus(gate_a + dt_bias)
    beta = jax.nn.sigmoid(gate_b.astype(jnp.float32).reshape(b, t, 2, DN_HEADS))
    o_f, s_f = gated_delta_chunked(q, k, v, log_decay[:, :, 0], beta[:, :, 0], s0_fwd)
    o_b, s_b = gated_delta_chunked(flip(q), flip(k), flip(v), flip(log_decay[:, :, 1]),
                                   flip(beta[:, :, 1]), s0_bwd)
    zg = jax.nn.silu(z.reshape(b, t, DN_HEADS, DN_HEAD_V).astype(jnp.float32))
    o = rms_norm(o_f + flip(o_b), norm_g) * zg
    return o.reshape(b, t, hv).astype(h.dtype) @ w_o, s_f, s_b


def setup_inputs(seed: int = 0) -> dict:
    key = jax.random.key(seed)
    ks = iter(jax.random.split(key, 48))
    d = D_MODEL
    hk, hv = DN_HEADS * DN_HEAD_K, DN_HEADS * DN_HEAD_V
    at_cols = (AT_HEADS + 2 * AT_KV_HEADS) * AT_HEAD

    def nrm(shape, scale):
        return scale * jax.random.normal(next(ks), shape, jnp.float32)

    def unif(shape, lo, hi):
        return jax.random.uniform(next(ks), shape, jnp.float32, lo, hi)

    inp = {}
    inp['x_prompt'] = nrm((BATCH, SEQ, d), 1.0)
    inp['x_sample'] = nrm((DEC_BATCH, DEC_SEQ, d), 1.0)
    inp['state_rwkv'] = nrm((DEC_BATCH, N_LAYERS_A, 2, RW_HEADS, RW_HEAD, RW_HEAD), 0.5)
    inp['cache_k'] = nrm((DEC_BATCH, N_LAYERS_B, PAST_LEN, AT_KV_HEADS, AT_HEAD), 1.0)
    inp['cache_v'] = nrm((DEC_BATCH, N_LAYERS_B, PAST_LEN, AT_KV_HEADS, AT_HEAD), 1.0)
    inp['state_delta'] = nrm((DEC_BATCH, N_LAYERS_C, 2, DN_HEADS, DN_HEAD_K, DN_HEAD_V), 0.1)
    inp['c'] = nrm((DEC_BATCH, d), 1.0)
    inp['c_ctx'] = nrm((d,), 1.0)
    inp['w_mod'] = nrm((DEPTH, d, N_MOD * d), 0.5 * d ** -0.5)
    inp['b_mod'] = nrm((DEPTH, N_MOD * d), 0.02)
    inp['ln_g'] = 1.0 + nrm((DEPTH, 2, d), 0.02)
    inp['ln_b'] = nrm((DEPTH, 2, d), 0.02)
    inp['w_fc1'] = nrm((DEPTH, d, D_FF), d ** -0.5)
    inp['w_fc2'] = nrm((DEPTH, D_FF, d), DEEPNORM_BETA * D_FF ** -0.5)
    inp['rw_mu'] = unif((N_LAYERS_A, 6, d), 0.0, 1.0)
    inp['rw_wrkv'] = nrm((N_LAYERS_A, 3, d, d), d ** -0.5)
    inp['rw_w0'] = unif((N_LAYERS_A, 2, d), -4.0, 1.0)
    inp['rw_w1'] = nrm((N_LAYERS_A, 2, d, RW_DECAY_RANK), d ** -0.5)
    inp['rw_w2'] = nrm((N_LAYERS_A, 2, RW_DECAY_RANK, d), 0.3 * RW_DECAY_RANK ** -0.5)
    inp['rw_a0'] = nrm((N_LAYERS_A, 2, d), 0.3)
    inp['rw_a1'] = nrm((N_LAYERS_A, 2, d, RW_ICL_RANK), d ** -0.5)
    inp['rw_a2'] = nrm((N_LAYERS_A, 2, RW_ICL_RANK, d), 0.3 * RW_ICL_RANK ** -0.5)
    inp['rw_g1'] = nrm((N_LAYERS_A, d, RW_GATE_RANK), d ** -0.5)
    inp['rw_g2'] = nrm((N_LAYERS_A, RW_GATE_RANK, d), RW_GATE_RANK ** -0.5)
    inp['rw_kk'] = 0.85 + nrm((N_LAYERS_A, d), 0.02)
    inp['rw_ka'] = 1.0 + nrm((N_LAYERS_A, d), 0.02)
    inp['rw_rk'] = nrm((N_LAYERS_A, RW_HEADS, RW_HEAD), 0.1)
    inp['rw_gn_g'] = 1.0 + nrm((N_LAYERS_A, d), 0.02)
    inp['rw_gn_b'] = nrm((N_LAYERS_A, d), 0.02)
    inp['rw_wo'] = nrm((N_LAYERS_A, d, d), DEEPNORM_BETA * d ** -0.5)
    inp['at_wqkv'] = nrm((N_LAYERS_B, d, at_cols), d ** -0.5)
    inp['at_qn'] = 1.0 + nrm((N_LAYERS_B, AT_HEAD), 0.02)
    inp['at_kn'] = 1.0 + nrm((N_LAYERS_B, AT_HEAD), 0.02)
    inp['at_wo'] = nrm((N_LAYERS_B, AT_HEADS * AT_HEAD, d), DEEPNORM_BETA * (AT_HEADS * AT_HEAD) ** -0.5)
    inp['dn_win'] = jnp.concatenate([
        nrm((N_LAYERS_C, d, 2 * hk + 2 * hv), d ** -0.5),
        nrm((N_LAYERS_C, d, 2 * DN_HEADS), 0.1 * d ** -0.5),
        nrm((N_LAYERS_C, d, 2 * DN_HEADS), d ** -0.5)], axis=-1)
    inp['dn_conv'] = nrm((N_LAYERS_C, DN_CONV, 2 * hk + hv), DN_CONV ** -0.5)
    inp['dn_alog'] = jnp.log(unif((N_LAYERS_C, 2, DN_HEADS), 1.0, 16.0))
    dt = jnp.exp(unif((N_LAYERS_C, 2, DN_HEADS), math.log(1e-3), math.log(1e-1)))
    inp['dn_dtb'] = dt + jnp.log(-jnp.expm1(-dt))
    inp['dn_ng'] = 1.0 + nrm((N_LAYERS_C, DN_HEAD_V), 0.02)
    inp['dn_wo'] = nrm((N_LAYERS_C, hv, d), DEEPNORM_BETA * hv ** -0.5)
    return inp


def reference(x_prompt, x_sample, state_rwkv, cache_k, cache_v, state_delta, c, c_ctx,
              w_mod, b_mod, ln_g, ln_b, w_fc1, w_fc2,
              rw_mu, rw_wrkv, rw_w0, rw_w1, rw_w2, rw_a0, rw_a1, rw_a2, rw_g1, rw_g2,
              rw_kk, rw_ka, rw_rk, rw_gn_g, rw_gn_b, rw_wo,
              at_wqkv, at_qn, at_kn, at_wo,
              dn_win, dn_conv, dn_alog, dn_dtb, dn_ng, dn_wo):
    xp, xs = x_prompt, x_sample
    n_ctx = xp.shape[0]
    new_rwkv, new_k, new_v, new_delta = [], [], [], []
    for l in range(DEPTH):
        kind, j = l % N_MIXERS, l // N_MIXERS
        mp = modulation(c_ctx, w_mod[l], b_mod[l])
        ms = modulation(c, w_mod[l], b_mod[l])
        hp = modulate(xp, mp[0], mp[1])
        hs = modulate(xs, ms[0], ms[1])
        if kind == 0:
            prm = (rw_mu[j], rw_wrkv[j], rw_w0[j], rw_w1[j], rw_w2[j], rw_a0[j], rw_a1[j],
                   rw_a2[j], rw_g1[j], rw_g2[j], rw_kk[j], rw_ka[j], rw_rk[j], rw_gn_g[j],
                   rw_gn_b[j], rw_wo[j])
            zero = jnp.zeros((n_ctx, RW_HEADS, RW_HEAD, RW_HEAD), jnp.float32)
            dp, s_f, s_b = rwkv7_mixer(hp, zero, zero, *prm)
            ds, _, _ = rwkv7_mixer(hs, state_rwkv[:, j, 0], state_rwkv[:, j, 1], *prm)
            new_rwkv.append(jnp.stack([s_f, s_b], axis=1).astype(xp.dtype))
        elif kind == 1:
            prm = (at_wqkv[j], at_qn[j], at_kn[j], at_wo[j])
            dp, kp, vp = attention_context(hp, *prm)
            ds = attention_latent(hs, cache_k[:, j], cache_v[:, j], *prm)
            new_k.append(kp)
            new_v.append(vp)
        else:
            prm = (dn_win[j], dn_conv[j], dn_alog[j], dn_dtb[j], dn_ng[j], dn_wo[j])
            zero = jnp.zeros((n_ctx, DN_HEADS, DN_HEAD_K, DN_HEAD_V), jnp.float32)
            dp, s_f, s_b = deltanet_mixer(hp, zero, zero, *prm)
            ds, _, _ = deltanet_mixer(hs, state_delta[:, j, 0], state_delta[:, j, 1], *prm)
            new_delta.append(jnp.stack([s_f, s_b], axis=1).astype(xp.dtype))
        xp = deepnorm_residual(xp, dp, mp[2], ln_g[l, 0], ln_b[l, 0])
        xs = deepnorm_residual(xs, ds, ms[2], ln_g[l, 0], ln_b[l, 0])
        hp = modulate(xp, mp[3], mp[4])
        hs = modulate(xs, ms[3], ms[4])
        xp = deepnorm_residual(xp, squared_relu_mlp(hp, w_fc1[l], w_fc2[l]), mp[5], ln_g[l, 1], ln_b[l, 1])
        xs = deepnorm_residual(xs, squared_relu_mlp(hs, w_fc1[l], w_fc2[l]), ms[5], ln_g[l, 1], ln_b[l, 1])
    new_state_rwkv = jnp.stack(new_rwkv, axis=1)
    new_cache_k = jnp.stack(new_k, axis=1)
    new_cache_v = jnp.stack(new_v, axis=1)
    new_state_delta = jnp.stack(new_delta, axis=1)
    return (xp, xs, new_state_rwkv, new_cache_k, new_cache_v, new_state_delta)
```

```python
import functools

import jax
import jax.numpy as jnp
from jax import lax
from jax.experimental import pallas as pl
from jax.experimental.pallas import tpu as pltpu

F32 = jnp.float32
BF16 = jnp.bfloat16

SEG = 256
CH = 64
LANES = 128
SUBLANES = 8
VMEM_LIMIT = 48 * 1024 * 1024

N_MOD = 6
LN_EPS = 1e-5
RW_HEAD = 64
RW_GN_EPS = 64e-5
AT_HEAD = 64
AT_GROUP = 4
GRID_W = 64
ROPE_THETA = 10000.0
DN_HEADS = 8


def _cparams(*sem):
    return pltpu.CompilerParams(dimension_semantics=sem, vmem_limit_bytes=VMEM_LIMIT)


def _dot(a, b):
    return jnp.dot(a, b, preferred_element_type=F32)


def _dot_nt(a, b):
    return lax.dot_general(a, b, (((1,), (1,)), ((), ())), preferred_element_type=F32)


def _dot_tn(a, b):
    return lax.dot_general(a, b, (((0,), (0,)), ((), ())), preferred_element_type=F32)


def _bf(x):
    return x.astype(BF16)


def _split3(x):
    hi = _bf(x)
    r1 = x - hi.astype(F32)
    mid = _bf(r1)
    lo = _bf(r1 - mid.astype(F32))
    return hi, mid, lo


def _dot_mask_exact(mask_bf, x):
    hi, mid, lo = _split3(x)
    return _dot(mask_bf, hi) + _dot(mask_bf, mid) + _dot(mask_bf, lo)


def _segsum(x, seg):
    w = x.shape[1]
    li = lax.broadcasted_iota(jnp.int32, (LANES, LANES), 0) // seg
    lj = lax.broadcasted_iota(jnp.int32, (LANES, LANES), 1) // seg
    bd = (li == lj).astype(BF16)
    outs = []
    for j in range(w // LANES):
        xs = x[:, LANES * j:LANES * (j + 1)]
        hi = _bf(xs)
        lo = _bf(xs - hi.astype(F32))
        outs.append(_dot(hi, bd) + _dot(lo, bd))
    return outs[0] if len(outs) == 1 else jnp.concatenate(outs, axis=1)


def _softplus(x):
    return jnp.maximum(x, 0.0) + jnp.log1p(jnp.exp(-jnp.abs(x)))


def _sigmoid(x):
    return 1.0 / (1.0 + jnp.exp(-x))


def _seg_flags(i, bp, parts):
    j = (i - bp) % parts
    is_ctx = i < bp
    return jnp.logical_or(is_ctx, j == 0), jnp.logical_or(is_ctx, j == parts - 1)


def _shift_prev(x, xp8, is_start):
    rolled = pltpu.roll(x, 1, 0)
    row0 = jnp.where(is_start, 0.0, xp8[SUBLANES - 1:SUBLANES, :])
    rid = lax.broadcasted_iota(jnp.int32, x.shape, 0)
    return jnp.where(rid == 0, row0, rolled)


def _shift_next(x, xn8, is_end):
    n = x.shape[0]
    rolled = pltpu.roll(x, n - 1, 0)
    rowl = jnp.where(is_end, 0.0, xn8[0:1, :])
    rid = lax.broadcasted_iota(jnp.int32, x.shape, 0)
    return jnp.where(rid == n - 1, rowl, rolled)


def _row_spec(w, col=0):
    return pl.BlockSpec((SEG, w), lambda i: (i, col))


def _halo_specs(w, m_rows):
    per = SEG // SUBLANES
    last = m_rows // SUBLANES - 1
    return (pl.BlockSpec((SUBLANES, w), lambda i: (jnp.maximum(i * per - 1, 0), 0)),
            pl.BlockSpec((SUBLANES, w), lambda i: (jnp.minimum((i + 1) * per, last), 0)))


def _const_spec(shape):
    nd = len(shape)
    return pl.BlockSpec(shape, lambda i: (0,) * nd)


def _segvec_spec(k, w):
    return pl.BlockSpec((1, k, w), lambda i: (i, 0, 0))


def _mm_kernel(x_ref, w_ref, o_ref, *, act):
    acc = _dot(_bf(x_ref[...]), w_ref[...])
    if act == "relu2":
        acc = jnp.square(jnp.maximum(acc, 0.0))
    o_ref[...] = acc.astype(o_ref.dtype)


def _mm(x, w, *, act=None, out_dtype=F32, tm=512, tn=512):
    m, k = x.shape
    n = w.shape[1]
    tn = min(tn, n)
    assert m % tm == 0 and n % tn == 0
    return pl.pallas_call(
        functools.partial(_mm_kernel, act=act),
        grid=(n // tn, m // tm),
        in_specs=[pl.BlockSpec((tm, k), lambda j, i: (i, 0)),
                  pl.BlockSpec((k, tn), lambda j, i: (0, j))],
        out_specs=pl.BlockSpec((tm, tn), lambda j, i: (i, j)),
        out_shape=jax.ShapeDtypeStruct((m, n), out_dtype),
        compiler_params=_cparams("parallel", "parallel"),
    )(x, w)


def _lowrank_kernel(x_ref, a_ref, b_ref, o_ref, *, act):
    t = _dot(x_ref[...], a_ref[...])
    if act == "tanh":
        t = jnp.tanh(t)
    elif act == "sigmoid":
        t = _sigmoid(t)
    o_ref[...] = _dot(_bf(t), b_ref[...])


def _lowrank(x, a, b, *, act=None, tm=512):
    m, k = x.shape
    r = a.shape[1]
    n = b.shape[1]
    return pl.pallas_call(
        functools.partial(_lowrank_kernel, act=act),
        grid=(m // tm,),
        in_specs=[pl.BlockSpec((tm, k), lambda i: (i, 0)),
                  pl.BlockSpec((k, r), lambda i: (0, 0)),
                  pl.BlockSpec((r, n), lambda i: (0, 0))],
        out_specs=pl.BlockSpec((tm, n), lambda i: (i, 0)),
        out_shape=jax.ShapeDtypeStruct((m, n), F32),
        compiler_params=_cparams("parallel"),
    )(x, a, b)


def _mod_kernel(c_ref, w_ref, b_ref, o_ref):
    c = c_ref[...]
    s = c * _sigmoid(c)
    o_ref[0] = _dot(_bf(s), _bf(w_ref[0])) + b_ref[0]


def _modulation(cond, w_mod, b_mod, tn=1536):
    depth, d, n = w_mod.shape
    return pl.pallas_call(
        _mod_kernel,
        grid=(depth, n // tn),
        in_specs=[pl.BlockSpec((SUBLANES, d), lambda l, j: (0, 0)),
                  pl.BlockSpec((1, d, tn), lambda l, j: (l, 0, j)),
                  pl.BlockSpec((1, 1, tn), lambda l, j: (l, 0, j))],
        out_specs=pl.BlockSpec((1, SUBLANES, tn), lambda l, j: (l, 0, j)),
        out_shape=jax.ShapeDtypeStruct((depth, SUBLANES, n), F32),
        compiler_params=_cparams("parallel", "parallel"),
    )(cond, w_mod, b_mod.reshape(depth, 1, n))


def _premod_kernel(x_ref, m_ref, h_ref):
    h = x_ref[...] * (1.0 + m_ref[0, 1:2, :]) + m_ref[0, 0:1, :]
    h_ref[...] = h.astype(h_ref.dtype)


def _premod(x, modseg, out_dtype):
    m, d = x.shape
    return pl.pallas_call(
        _premod_kernel,
        grid=(m // SEG,),
        in_specs=[_row_spec(d), _segvec_spec(N_MOD, d)],
        out_specs=_row_spec(d),
        out_shape=jax.ShapeDtypeStruct((m, d), out_dtype),
        compiler_params=_cparams("parallel"),
    )(x, modseg)


def _ln_kernel(x_ref, d_ref, m_ref, mn_ref, g_ref, b_ref, xo_ref, *rest, alpha, gate_idx, sh_idx):
    y = alpha * x_ref[...] + m_ref[0, gate_idx:gate_idx + 1, :] * d_ref[...]
    mu = jnp.mean(y, axis=-1, keepdims=True)
    yc = y - mu
    var = jnp.mean(yc * yc, axis=-1, keepdims=True)
    xn = yc * lax.rsqrt(var + LN_EPS) * g_ref[...] + b_ref[...]
    xo_ref[...] = xn
    if rest:
        h = xn * (1.0 + mn_ref[0, sh_idx + 1:sh_idx + 2, :]) + mn_ref[0, sh_idx:sh_idx + 1, :]
        rest[0][...] = h.astype(rest[0].dtype)


def _deepnorm_ln(x, delta, modseg, modseg_next, g, b, *, alpha, gate_idx, sh_idx, h_dtype):
    m, d = x.shape
    out_shape = [jax.ShapeDtypeStruct((m, d), F32)]
    out_specs = [_row_spec(d)]
    if h_dtype is not None:
        out_shape.append(jax.ShapeDtypeStruct((m, d), h_dtype))
        out_specs.append(_row_spec(d))
    res = pl.pallas_call(
        functools.partial(_ln_kernel, alpha=alpha, gate_idx=gate_idx, sh_idx=sh_idx),
        grid=(m // SEG,),
        in_specs=[_row_spec(d), _row_spec(d), _segvec_spec(N_MOD, d), _segvec_spec(N_MOD, d),
                  _const_spec((1, d)), _const_spec((1, d))],
        out_specs=out_specs,
        out_shape=out_shape,
        compiler_params=_cparams("parallel"),
    )(x, delta, modseg, modseg_next, g.reshape(1, d), b.reshape(1, d))
    return (res[0], res[1]) if h_dtype is not None else (res[0], None)


def _merge_masks(c):
    ti = lax.broadcasted_iota(jnp.int32, (c, c), 0)
    si = lax.broadcasted_iota(jnp.int32, (c, c), 1)
    masks = []
    b = 1
    while b < c:
        masks.append(jnp.logical_and(ti // (2 * b) == si // (2 * b), ti // b != si // b))
        b *= 2
    return masks


def _tri_inverse(low, masks, eye):
    t = eye - jnp.where(masks[0], low, 0.0)
    for mk in masks[1:]:
        mo = _bf(jnp.where(mk, low, 0.0))
        tb = _bf(t)
        t = t - _dot(tb, _bf(_dot(mo, tb)))
    return t


def _scan_masks(e):
    ti = lax.broadcasted_iota(jnp.int32, (CH, CH), 0)
    si = lax.broadcasted_iota(jnp.int32, (CH, CH), 1)
    ahead = (ti - si) * (1 - 2 * e)
    incl = ahead >= 0
    strict = ahead > 0
    eye = (ti == si).astype(F32)
    return incl, strict, eye


def _chunk_row(base, nc):
    def index(e, b, n):
        return base + b * nc + n + e * (nc - 1 - 2 * n)
    return index


def _rw_pre_kernel(h_ref, hp_ref, hn_ref, mu_ref, *outs, bp, parts):
    is_start, is_end = _seg_flags(pl.program_id(0), bp, parts)
    h = h_ref[...]
    xx = 0.5 * (_shift_prev(h, hp_ref[...], is_start) + _shift_next(h, hn_ref[...], is_end)) - h
    for j, o_ref in enumerate(outs):
        o_ref[...] = _bf(h + xx * mu_ref[j:j + 1, :])


def _rw_pre(h, mu, bp, parts):
    m, d = h.shape
    hp, hn = _halo_specs(d, m)
    return pl.pallas_call(
        functools.partial(_rw_pre_kernel, bp=bp, parts=parts),
        grid=(m // SEG,),
        in_specs=[_row_spec(d), hp, hn, _const_spec((6, d))],
        out_specs=[_row_spec(d)] * 6,
        out_shape=[jax.ShapeDtypeStruct((m, d), BF16)] * 6,
        compiler_params=_cparams("parallel"),
    )(h, h, h, mu)


def _rw_mid_kernel(r_ref, k_ref, v_ref, wl_ref, al_ref, w0_ref, a0_ref, kk_ref, ka_ref, rk_ref,
                   ld_ref, be_ref, kd_ref, alpha_ref, bonus_ref):
    d = r_ref.shape[1]
    r = r_ref[...]
    k = k_ref[...]
    kkn = k * kk_ref[...]
    kk = kkn * lax.rsqrt(_segsum(kkn * kkn, RW_HEAD) + 1e-6)
    alpha_ref[...] = -kk
    bon = jnp.zeros_like(r)
    for e in range(2):
        cols = slice(e * d, (e + 1) * d)
        w_log = -_softplus(-(w0_ref[e:e + 1, :] + wl_ref[:, cols])) - 0.5
        ld_ref[:, cols] = -jnp.exp(w_log)
        a = _sigmoid(a0_ref[e:e + 1, :] + al_ref[:, cols])
        be_ref[:, cols] = kk * a
        kd = k * (1.0 + (a - 1.0) * ka_ref[...])
        kd_ref[:, cols] = kd
        bon = bon + r * kd * rk_ref[...]
    bonus_ref[...] = _segsum(bon, RW_HEAD) * v_ref[...]


def _rw_mid(r, k, v, wl, al, w0, a0, kk, ka, rk):
    m, d = r.shape
    return pl.pallas_call(
        _rw_mid_kernel,
        grid=(m // SEG,),
        in_specs=[_row_spec(d)] * 3 + [_row_spec(2 * d)] * 2
        + [_const_spec((2, d))] * 2 + [_const_spec((1, d))] * 3,
        out_specs=[_row_spec(2 * d)] * 3 + [_row_spec(d)] * 2,
        out_shape=[jax.ShapeDtypeStruct((m, 2 * d), F32)] * 3 + [jax.ShapeDtypeStruct((m, d), F32)] * 2,
        compiler_params=_cparams("parallel"),
    )(r, k, v, wl, al, w0, a0, kk.reshape(1, d), ka.reshape(1, d), rk.reshape(1, d))


def _rw_scan_kernel(r_ref, ld_ref, al_ref, be_ref, kd_ref, v_ref, s0_ref, o_ref, sT_ref, s_scr, *, nh, hd):
    e = pl.program_id(0)
    n = pl.program_id(2)

    @pl.when(n == 0)
    def _():
        s_scr[...] = s0_ref[0, 0]

    incl, strict, eye = _scan_masks(e)
    masks = _merge_masks(CH)
    ld = ld_ref[...]
    cs = _dot_mask_exact(incl.astype(BF16), ld)
    tot = jnp.sum(ld, axis=0, keepdims=True)
    g_end = jnp.exp(tot - cs)
    g_inv = jnp.exp(-cs)
    g_tot = jnp.exp(tot)
    be = be_ref[...]
    kd = kd_ref[...]
    ab = _bf(al_ref[...] * jnp.exp(cs - ld))
    qb = _bf(r_ref[...] * jnp.exp(cs))
    bt = _bf(be * g_inv)
    kt = _bf(kd * g_inv)
    b_end = _bf(be * g_end)
    k_end = _bf(kd * g_end)
    vb = _bf(v_ref[...])
    for h in range(nh):
        sl = slice(h * hd, (h + 1) * hd)
        s_old = s_scr[h]
        sb = _bf(s_old)
        aq = jnp.concatenate([ab[:, sl], qb[:, sl]], axis=0)
        xb = _dot_nt(aq, bt[:, sl])
        xk = _dot_nt(aq, kt[:, sl])
        t_inv = _tri_inverse(jnp.where(strict, -xb[:CH], 0.0), masks, eye)
        a_ak = _bf(jnp.where(strict, xk[:CH], 0.0))
        a_qb = _bf(jnp.where(incl, xb[CH:], 0.0))
        a_qk = _bf(jnp.where(incl, xk[CH:], 0.0))
        vh = vb[:, sl]
        rhs = _dot_nt(ab[:, sl], sb) + _dot(a_ak, vh)
        u = _bf(_dot(_bf(t_inv), _bf(rhs)))
        o_ref[:, sl] = _dot_nt(qb[:, sl], sb) + _dot(a_qb, u) + _dot(a_qk, vh)
        s_scr[h] = s_old * g_tot[:, sl] + _dot_tn(u, b_end[:, sl]) + _dot_tn(vh, k_end[:, sl])

    @pl.when(n == pl.num_programs(2) - 1)
    def _():
        sT_ref[0, 0] = s_scr[...]


def _rw_scan(r, ld, alpha, be, kd, v, s0, *, base, nc):
    m, d = r.shape
    _, b, nh, hd, _ = s0.shape
    row = _chunk_row(base, nc)
    out_row = _chunk_row(0, nc)
    shared = pl.BlockSpec((CH, d), lambda e, bb, n: (row(e, bb, n), 0))
    perdir = pl.BlockSpec((CH, d), lambda e, bb, n: (row(e, bb, n), e))
    out = pl.BlockSpec((CH, d), lambda e, bb, n: (out_row(e, bb, n), e))
    st = pl.BlockSpec((1, 1, nh, hd, hd), lambda e, bb, n: (e, bb, 0, 0, 0))
    return pl.pallas_call(
        functools.partial(_rw_scan_kernel, nh=nh, hd=hd),
        grid=(2, b, nc),
        in_specs=[shared, perdir, shared, perdir, perdir, shared, st],
        out_specs=[out, st],
        out_shape=[jax.ShapeDtypeStruct((b * nc * CH, 2 * d), F32), jax.ShapeDtypeStruct(s0.shape, F32)],
        scratch_shapes=[pltpu.VMEM((nh, hd, hd), F32)],
        compiler_params=_cparams("parallel", "parallel", "arbitrary"),
    )(r, ld, alpha, be, kd, v, s0)


def _rw_post_kernel(o_ref, bonus_ref, g_ref, gng_ref, gnb_ref, y_ref):
    d = y_ref.shape[1]
    o = o_ref[:, :d] + o_ref[:, d:]
    oc = o - _segsum(o, RW_HEAD) * (1.0 / RW_HEAD)
    var = _segsum(oc * oc, RW_HEAD) * (1.0 / RW_HEAD)
    y = oc * lax.rsqrt(var + RW_GN_EPS) * gng_ref[...] + gnb_ref[...]
    y_ref[...] = _bf((y + bonus_ref[...]) * g_ref[...])


def _rw_post(o, bonus, g, gn_g, gn_b):
    m, d = bonus.shape
    return pl.pallas_call(
        _rw_post_kernel,
        grid=(m // SEG,),
        in_specs=[_row_spec(2 * d), _row_spec(d), _row_spec(d), _const_spec((1, d)), _const_spec((1, d))],
        out_specs=_row_spec(d),
        out_shape=jax.ShapeDtypeStruct((m, d), BF16),
        compiler_params=_cparams("parallel"),
    )(o, bonus, g, gn_g.reshape(1, d), gn_b.reshape(1, d))


def _rwkv_mixer(h, state, p, dims):
    bp, bs, parts = dims
    m, d = h.shape
    xr, xw, xk, xv, xa, xg = _rw_pre(h, p["mu"], bp, parts)
    r = _mm(xr, p["wr"])
    k = _mm(xk, p["wk"])
    v = _mm(xv, p["wv"])
    wl = _lowrank(xw, p["w1"], p["w2"], act="tanh")
    al = _lowrank(xa, p["a1"], p["a2"])
    g = _lowrank(xg, p["g1"], p["g2"], act="sigmoid")
    ld, be, kd, alpha, bonus = _rw_mid(r, k, v, wl, al, p["w0"], p["a0"], p["kk"], p["ka"], p["rk"])
    nh = d // RW_HEAD
    ncp = SEG // CH
    zero = jnp.zeros((2, bp, nh, RW_HEAD, RW_HEAD), F32)
    o_p, s_p = _rw_scan(r, ld, alpha, be, kd, v, zero, base=0, nc=ncp)
    s0 = jnp.swapaxes(state, 0, 1)
    o_s, _ = _rw_scan(r, ld, alpha, be, kd, v, s0, base=bp * ncp, nc=parts * ncp)
    o = jnp.concatenate([o_p, o_s], axis=0)
    y = _rw_post(o, bonus, g, p["gn_g"], p["gn_b"])
    return _mm(y, p["wo"]), jnp.swapaxes(s_p, 0, 1)


def _rope(y, cos, sin):
    w = y.shape[1]
    up = pltpu.roll(y, w - AT_HEAD // 4, 1)
    dn = pltpu.roll(y, AT_HEAD // 4, 1)
    lane = lax.broadcasted_iota(jnp.int32, y.shape, 1)
    first = (lane % (AT_HEAD // 2)) < (AT_HEAD // 4)
    return y * cos + jnp.where(first, up, dn) * sin


def _at_pre_kernel(qkv_ref, cos_ref, sin_ref, qg_ref, kg_ref, q_ref, kr_ref, kn_ref, *, dq, dk):
    q = qkv_ref[:, :dq]
    k = qkv_ref[:, dq:dq + dk]
    qn = q * lax.rsqrt(_segsum(q * q, AT_HEAD) * (1.0 / AT_HEAD) + 1e-6) * qg_ref[...]
    kn = k * lax.rsqrt(_segsum(k * k, AT_HEAD) * (1.0 / AT_HEAD) + 1e-6) * kg_ref[...]
    kn_ref[...] = kn
    q_ref[...] = _bf(_rope(qn, cos_ref[...], sin_ref[...]))
    kr_ref[...] = _bf(_rope(kn, cos_ref[:, :dk], sin_ref[:, :dk]))


def _at_pre(qkv, cos_t, sin_t, qg, kg, bp, parts, dq, dk):
    m, w = qkv.shape
    tab = pl.BlockSpec((SEG, dq), lambda i: (jnp.where(i < bp, 0, 1 + (i - bp) % parts), 0))
    return pl.pallas_call(
        functools.partial(_at_pre_kernel, dq=dq, dk=dk),
        grid=(m // SEG,),
        in_specs=[_row_spec(w), tab, tab, _const_spec((1, dq)), _const_spec((1, dk))],
        out_specs=[_row_spec(dq), _row_spec(dk), _row_spec(dk)],
        out_shape=[jax.ShapeDtypeStruct((m, dq), BF16), jax.ShapeDtypeStruct((m, dk), BF16),
                   jax.ShapeDtypeStruct((m, dk), F32)],
        compiler_params=_cparams("parallel"),
    )(qkv, cos_t, sin_t, qg, kg)


def _attn_kernel(q_ref, k_ref, v_ref, o_ref, *, scale):
    nkv = k_ref.shape[2] // AT_HEAD
    for g in range(nkv):
        kg = k_ref[0, :, g * AT_HEAD:(g + 1) * AT_HEAD]
        vg = v_ref[0, :, g * AT_HEAD:(g + 1) * AT_HEAD]
        for j in range(AT_GROUP):
            sl = slice((g * AT_GROUP + j) * AT_HEAD, (g * AT_GROUP + j + 1) * AT_HEAD)
            s = _dot_nt(q_ref[:, sl], kg) * scale
            p = jnp.exp(s - jnp.max(s, axis=-1, keepdims=True))
            l = jnp.sum(p, axis=-1, keepdims=True)
            o_ref[:, sl] = (_dot(_bf(p), vg) / l).astype(o_ref.dtype)


def _attention(q, k, v, *, base, nq):
    dq = q.shape[1]
    b, tk, dk = k.shape
    kv = pl.BlockSpec((1, tk, dk), lambda bb, j: (bb, 0, 0))
    return pl.pallas_call(
        functools.partial(_attn_kernel, scale=AT_HEAD ** -0.5),
        grid=(b, nq),
        in_specs=[pl.BlockSpec((SEG, dq), lambda bb, j: (base + bb * nq + j, 0)), kv, kv],
        out_specs=pl.BlockSpec((SEG, dq), lambda bb, j: (bb * nq + j, 0)),
        out_shape=jax.ShapeDtypeStruct((b * nq * SEG, dq), BF16),
        compiler_params=_cparams("parallel", "parallel"),
    )(q, k, v)


def _rope_tables(t, heads):
    rows = t // GRID_W
    row = jnp.repeat(jnp.arange(rows), GRID_W)
    col = jnp.tile(jnp.arange(GRID_W), rows)
    n_freq = AT_HEAD // 4
    inv_freq = ROPE_THETA ** (-jnp.arange(n_freq, dtype=F32) / n_freq)
    ang = jnp.stack([row, col], axis=-1).astype(F32)[:, :, None] * inv_freq
    cos = jnp.repeat(jnp.cos(ang)[:, :, None, :], 2, axis=2).reshape(t, AT_HEAD)
    sin = jnp.sin(ang)
    sin = jnp.stack([-sin, sin], axis=2).reshape(t, AT_HEAD)
    ident = (jnp.ones((SEG, AT_HEAD), F32), jnp.zeros((SEG, AT_HEAD), F32))
    cos = jnp.tile(jnp.concatenate([ident[0], cos], axis=0), (1, heads))
    sin = jnp.tile(jnp.concatenate([ident[1], sin], axis=0), (1, heads))
    return cos, sin


def _attention_mixer(h, cache_k, cache_v, p, dims):
    bp, bs, parts = dims
    m, d = h.shape
    dk = cache_k.shape[-1] * cache_k.shape[-2]
    heads = d // AT_HEAD
    qkv = _mm(h, p["wqkv"])
    cos_t, sin_t = _rope_tables(parts * SEG, heads)
    q, kr, kn = _at_pre(qkv, cos_t, sin_t, jnp.tile(p["qn"], heads).reshape(1, d),
                        jnp.tile(p["kn"], dk // AT_HEAD).reshape(1, dk), bp, parts, d, dk)
    v = qkv[:, d + dk:]
    np_ = bp * SEG
    o_p = _attention(q, kr[:np_].reshape(bp, SEG, dk), _bf(v[:np_]).reshape(bp, SEG, dk), base=0, nq=1)
    k_s = jnp.concatenate([_bf(cache_k.reshape(bs, -1, dk)), kr[np_:].reshape(bs, parts * SEG, dk)], axis=1)
    v_s = jnp.concatenate([_bf(cache_v.reshape(bs, -1, dk)), _bf(v[np_:]).reshape(bs, parts * SEG, dk)], axis=1)
    o_s = _attention(q, k_s, v_s, base=bp, nq=parts)
    delta = _mm(jnp.concatenate([o_p, o_s], axis=0), p["wo"])
    return delta, kn[:np_], v[:np_]


def _dn_pre_kernel(x_ref, xp_ref, xn_ref, gt_ref, cw_ref, ga_ref, gb_ref, q_ref, k_ref, v_ref, gd_ref,
                   *, bp, parts, d, dk):
    is_start, is_end = _seg_flags(pl.program_id(0), bp, parts)
    x = x_ref[...]
    c = (cw_ref[0:1, :] * _shift_prev(x, xp_ref[...], is_start) + cw_ref[1:2, :] * x
         + cw_ref[2:3, :] * _shift_next(x, xn_ref[...], is_end))
    s = c * _sigmoid(c)
    q = s[:, :d]
    k = s[:, d:2 * d]
    q_ref[...] = q * lax.rsqrt(_segsum(q * q, dk) + 1e-6) * (dk ** -0.5)
    k_ref[...] = k * lax.rsqrt(_segsum(k * k, dk) + 1e-6)
    v_ref[...] = s[:, 2 * d:]
    gt = gt_ref[...]
    lane = lax.broadcasted_iota(jnp.int32, gt.shape, 1) % LANES
    gd_ref[...] = jnp.where(lane < DN_HEADS, -jnp.exp(ga_ref[...]) * _softplus(gt + gb_ref[...]), _sigmoid(gt))


def _dn_pre(proj, gates, conv_w, ga, gb, bp, parts, d, dk):
    m = proj.shape[0]
    w3 = 3 * d
    hp, hn = _halo_specs(w3, m)
    return pl.pallas_call(
        functools.partial(_dn_pre_kernel, bp=bp, parts=parts, d=d, dk=dk),
        grid=(m // SEG,),
        in_specs=[_row_spec(w3), hp, hn, _row_spec(2 * LANES), _const_spec((3, w3)),
                  _const_spec((1, 2 * LANES)), _const_spec((1, 2 * LANES))],
        out_specs=[_row_spec(d)] * 3 + [_row_spec(2 * LANES)],
        out_shape=[jax.ShapeDtypeStruct((m, d), F32)] * 3 + [jax.ShapeDtypeStruct((m, 2 * LANES), F32)],
        compiler_params=_cparams("parallel"),
    )(proj, proj, proj, gates, conv_w, ga, gb)


def _dn_scan_kernel(q_ref, k_ref, v_ref, gd_ref, s0_ref, o_ref, sT_ref, s_scr, *, nh, hd):
    e = pl.program_id(0)
    n = pl.program_id(2)

    @pl.when(n == 0)
    def _():
        s_scr[...] = s0_ref[0, 0]

    incl, strict, eye = _scan_masks(e)
    masks = _merge_masks(CH)
    ones = jnp.ones((CH, CH), BF16)
    gd = gd_ref[...]
    cs = _dot_mask_exact(incl.astype(BF16), gd)
    tot = jnp.sum(gd, axis=0, keepdims=True)
    e_cs = jnp.exp(cs)
    e_end = jnp.exp(tot - cs)
    e_tot = jnp.exp(tot)
    for h in range(nh):
        sl = slice(h * hd, (h + 1) * hd)
        g_col = jnp.broadcast_to(cs[:, h:h + 1], (CH, CH))
        g_row = _dot_mask_exact(ones, jnp.where(eye > 0, g_col, 0.0))
        dec = jnp.where(incl, jnp.exp(jnp.where(incl, g_col - g_row, 0.0)), 0.0)
        beta = gd[:, nh + h:nh + h + 1]
        kh = k_ref[:, sl]
        kb = kh * beta
        khb = _bf(kh)
        low = jnp.where(strict, _dot_nt(_bf(kb), khb) * dec, 0.0)
        t_inv = _bf(_tri_inverse(low, masks, eye))
        u = _dot(t_inv, _bf(v_ref[:, sl] * beta))
        w = _dot(t_inv, _bf(kb * e_cs[:, h:h + 1]))
        qh = q_ref[:, sl]
        qk = _bf(jnp.where(incl, _dot_nt(_bf(qh), khb) * dec, 0.0))
        s_old = s_scr[h]
        sb = _bf(s_old)
        v_new = _bf(u - _dot(_bf(w), sb))
        o_ref[:, sl] = _dot(_bf(qh * e_cs[:, h:h + 1]), sb) + _dot(qk, v_new)
        s_scr[h] = s_old * e_tot[:, h:h + 1] + _dot_tn(_bf(kh * e_end[:, h:h + 1]), v_new)

    @pl.when(n == pl.num_programs(2) - 1)
    def _():
        sT_ref[0, 0] = s_scr[...]


def _dn_scan(q, k, v, gd, s0, *, base, nc):
    m, d = q.shape
    _, b, nh, hd, _ = s0.shape
    row = _chunk_row(base, nc)
    out_row = _chunk_row(0, nc)
    shared = pl.BlockSpec((CH, d), lambda e, bb, n: (row(e, bb, n), 0))
    gates = pl.BlockSpec((CH, LANES), lambda e, bb, n: (row(e, bb, n), e))
    out = pl.BlockSpec((CH, d), lambda e, bb, n: (out_row(e, bb, n), e))
    st = pl.BlockSpec((1, 1, nh, hd, hd), lambda e, bb, n: (e, bb, 0, 0, 0))
    return pl.pallas_call(
        functools.partial(_dn_scan_kernel, nh=nh, hd=hd),
        grid=(2, b, nc),
        in_specs=[shared, shared, shared, gates, st],
        out_specs=[out, st],
        out_shape=[jax.ShapeDtypeStruct((b * nc * CH, 2 * d), F32), jax.ShapeDtypeStruct(s0.shape, F32)],
        scratch_shapes=[pltpu.VMEM((nh, hd, hd), F32)],
        compiler_params=_cparams("parallel", "parallel", "arbitrary"),
    )(q, k, v, gd, s0)


def _dn_post_kernel(o_ref, z_ref, ng_ref, y_ref, *, dv):
    d = y_ref.shape[1]
    o = o_ref[:, :d] + o_ref[:, d:]
    z = z_ref[...]
    y = o * lax.rsqrt(_segsum(o * o, dv) * (1.0 / dv) + 1e-6) * ng_ref[...]
    y_ref[...] = _bf(y * (z * _sigmoid(z)))


def _dn_post(o, proj, ng, d, dv):
    m = o.shape[0]
    return pl.pallas_call(
        functools.partial(_dn_post_kernel, dv=dv),
        grid=(m // SEG,),
        in_specs=[_row_spec(2 * d), _row_spec(d, col=3), _const_spec((1, d))],
        out_specs=_row_spec(d),
        out_shape=jax.ShapeDtypeStruct((m, d), BF16),
        compiler_params=_cparams("parallel"),
    )(o, proj, ng)


def _deltanet_mixer(h, state, p, dims):
    bp, bs, parts = dims
    m, d = h.shape
    dk = d // DN_HEADS
    proj = _mm(h, p["win"])
    gates = _mm(h, p["wg"], tn=2 * LANES)
    q, k, v, gd = _dn_pre(proj, gates, p["conv"], p["ga"], p["gb"], bp, parts, d, dk)
    ncp = SEG // CH
    zero = jnp.zeros((2, bp, DN_HEADS, dk, dk), F32)
    o_p, s_p = _dn_scan(q, k, v, gd, zero, base=0, nc=ncp)
    o_s, _ = _dn_scan(q, k, v, gd, jnp.swapaxes(state, 0, 1), base=bp * ncp, nc=parts * ncp)
    o = jnp.concatenate([o_p, o_s], axis=0)
    y = _dn_post(o, proj, jnp.tile(p["ng"], DN_HEADS).reshape(1, d), d, dk)
    return _mm(y, p["wo"]), jnp.swapaxes(s_p, 0, 1)


def _gate_columns(w_gate):
    dd = w_gate.shape[0]
    pad = jnp.zeros((dd, LANES - 2 * DN_HEADS), w_gate.dtype)
    cols = []
    for e in range(2):
        cols += [w_gate[:, e * DN_HEADS:(e + 1) * DN_HEADS],
                 w_gate[:, (2 + e) * DN_HEADS:(3 + e) * DN_HEADS], pad]
    return jnp.concatenate(cols, axis=1)


def _gate_lanes(x):
    pad = jnp.zeros((LANES - DN_HEADS,), x.dtype)
    return jnp.concatenate([x[0], pad, x[1], pad]).reshape(1, 2 * LANES)


def kernel(x_prompt, x_sample, state_rwkv, cache_k, cache_v, state_delta, c, c_ctx, w_mod, b_mod, ln_g, ln_b, w_fc1, w_fc2, rw_mu, rw_wrkv, rw_w0, rw_w1, rw_w2, rw_a0, rw_a1, rw_a2, rw_g1, rw_g2, rw_kk, rw_ka, rw_rk, rw_gn_g, rw_gn_b, rw_wo, at_wqkv, at_qn, at_kn, at_wo, dn_win, dn_conv, dn_alog, dn_dtb, dn_ng, dn_wo):
    bp, tp, d = x_prompt.shape
    bs, ts, _ = x_sample.shape
    assert tp == SEG and ts % SEG == 0 and bs + 1 <= SUBLANES
    parts = ts // SEG
    dims = (bp, bs, parts)
    depth = w_mod.shape[0]
    alpha = (2.0 * depth) ** 0.25

    x = jnp.concatenate([x_prompt.reshape(bp * tp, d), x_sample.reshape(bs * ts, d)], axis=0)
    cond = jnp.concatenate([c, c_ctx[None, :], jnp.zeros((SUBLANES - bs - 1, d), F32)], axis=0)
    mods = _modulation(cond, w_mod, b_mod)
    seg_row = jnp.array([bs] * bp + [b for b in range(bs) for _ in range(parts)], jnp.int32)
    modseg = [mods[l][seg_row].reshape(bp + bs * parts, N_MOD, d) for l in range(depth)]

    def mixer_in_dtype(l):
        return F32 if l % 3 == 0 else BF16

    new_rwkv, new_k, new_v, new_delta = [], [], [], []
    h = _premod(x, modseg[0], mixer_in_dtype(0))
    for l in range(depth):
        kind, j = l % 3, l // 3
        if kind == 0:
            z64 = jnp.zeros((RW_HEAD, d), F32)
            prm = dict(
                mu=rw_mu[j], wr=_bf(rw_wrkv[j, 0]), wk=_bf(rw_wrkv[j, 1]), wv=_bf(rw_wrkv[j, 2]),
                w1=_bf(jnp.concatenate([rw_w1[j, 0], rw_w1[j, 1]], axis=1)),
                w2=_bf(jnp.concatenate([jnp.concatenate([rw_w2[j, 0], z64], axis=1),
                                        jnp.concatenate([z64, rw_w2[j, 1]], axis=1)], axis=0)),
                a1=_bf(jnp.concatenate([rw_a1[j, 0], rw_a1[j, 1]], axis=1)),
                a2=_bf(jnp.concatenate([jnp.concatenate([rw_a2[j, 0], z64], axis=1),
                                        jnp.concatenate([z64, rw_a2[j, 1]], axis=1)], axis=0)),
                g1=_bf(rw_g1[j]), g2=_bf(rw_g2[j]), w0=rw_w0[j], a0=rw_a0[j],
                kk=rw_kk[j], ka=rw_ka[j], rk=rw_rk[j].reshape(-1), gn_g=rw_gn_g[j], gn_b=rw_gn_b[j],
                wo=_bf(rw_wo[j]))
            delta, s_new = _rwkv_mixer(h, state_rwkv[:, j], prm, dims)
            new_rwkv.append(s_new)
        elif kind == 1:
            prm = dict(wqkv=_bf(at_wqkv[j]), qn=at_qn[j], kn=at_kn[j], wo=_bf(at_wo[j]))
            delta, kp, vp = _attention_mixer(h, cache_k[:, j], cache_v[:, j], prm, dims)
            new_k.append(kp.reshape(bp, tp, -1, AT_HEAD))
            new_v.append(vp.reshape(bp, tp, -1, AT_HEAD))
        else:
            prm = dict(win=_bf(dn_win[j, :, :4 * d]), wg=_bf(_gate_columns(dn_win[j, :, 4 * d:])),
                       conv=dn_conv[j], ga=_gate_lanes(dn_alog[j]), gb=_gate_lanes(dn_dtb[j]),
                       ng=dn_ng[j], wo=_bf(dn_wo[j]))
            delta, s_new = _deltanet_mixer(h, state_delta[:, j], prm, dims)
            new_delta.append(s_new)
        x, h = _deepnorm_ln(x, delta, modseg[l], modseg[l], ln_g[l, 0], ln_b[l, 0],
                            alpha=alpha, gate_idx=2, sh_idx=3, h_dtype=BF16)
        u = _mm(h, _bf(w_fc1[l]), act="relu2", out_dtype=BF16)
        delta = _mm(u, _bf(w_fc2[l]))
        nxt = min(l + 1, depth - 1)
        x, h = _deepnorm_ln(x, delta, modseg[l], modseg[nxt], ln_g[l, 1], ln_b[l, 1],
                            alpha=alpha, gate_idx=5, sh_idx=0,
                            h_dtype=mixer_in_dtype(l + 1) if l + 1 < depth else None)
    y_prompt = x[:bp * tp].reshape(bp, tp, d)
    y_sample = x[bp * tp:].reshape(bs, ts, d)
    return (y_prompt, y_sample, jnp.stack(new_rwkv, axis=1), jnp.stack(new_k, axis=1),
            jnp.stack(new_v, axis=1), jnp.stack(new_delta, axis=1))
```

```python
import functools

import jax
import jax.numpy as jnp
from jax import lax
from jax.experimental import pallas as pl
from jax.experimental.pallas import tpu as pltpu

F32 = jnp.float32
BF16 = jnp.bfloat16

SEG = 256
CH = 64
LANES = 128
SUBLANES = 8
VMEM_LIMIT = 48 * 1024 * 1024

N_MOD = 6
LN_EPS = 1e-5
RW_HEAD = 64
RW_GN_EPS = 64e-5
AT_HEAD = 64
AT_GROUP = 4
GRID_W = 64
ROPE_THETA = 10000.0
DN_HEADS = 8


def _cparams(*sem):
    return pltpu.CompilerParams(dimension_semantics=sem, vmem_limit_bytes=VMEM_LIMIT)


def _dot(a, b):
    return jnp.dot(a, b, preferred_element_type=F32)


def _dot_nt(a, b):
    return lax.dot_general(a, b, (((1,), (1,)), ((), ())), preferred_element_type=F32)


def _dot_tn(a, b):
    return lax.dot_general(a, b, (((0,), (0,)), ((), ())), preferred_element_type=F32)


def _bf(x):
    return x.astype(BF16)


def _split3(x):
    hi = _bf(x)
    r1 = x - hi.astype(F32)
    mid = _bf(r1)
    lo = _bf(r1 - mid.astype(F32))
    return hi, mid, lo


def _dot_mask_exact(mask_bf, x):
    hi, mid, lo = _split3(x)
    return _dot(mask_bf, hi) + _dot(mask_bf, mid) + _dot(mask_bf, lo)


def _segsum(x, seg):
    w = x.shape[1]
    li = lax.broadcasted_iota(jnp.int32, (LANES, LANES), 0) // seg
    lj = lax.broadcasted_iota(jnp.int32, (LANES, LANES), 1) // seg
    bd = (li == lj).astype(BF16)
    outs = []
    for j in range(w // LANES):
        xs = x[:, LANES * j:LANES * (j + 1)]
        hi = _bf(xs)
        lo = _bf(xs - hi.astype(F32))
        outs.append(_dot(hi, bd) + _dot(lo, bd))
    return outs[0] if len(outs) == 1 else jnp.concatenate(outs, axis=1)


def _softplus(x):
    return jnp.maximum(x, 0.0) + jnp.log1p(jnp.exp(-jnp.abs(x)))


def _sigmoid(x):
    return 1.0 / (1.0 + jnp.exp(-x))


def _seg_flags(i, bp, parts):
    j = (i - bp) % parts
    is_ctx = i < bp
    return jnp.logical_or(is_ctx, j == 0), jnp.logical_or(is_ctx, j == parts - 1)


def _shift_prev(x, xp8, is_start):
    rolled = pltpu.roll(x, 1, 0)
    row0 = jnp.where(is_start, 0.0, xp8[SUBLANES - 1:SUBLANES, :])
    rid = lax.broadcasted_iota(jnp.int32, x.shape, 0)
    return jnp.where(rid == 0, row0, rolled)


def _shift_next(x, xn8, is_end):
    n = x.shape[0]
    rolled = pltpu.roll(x, n - 1, 0)
    rowl = jnp.where(is_end, 0.0, xn8[0:1, :])
    rid = lax.broadcasted_iota(jnp.int32, x.shape, 0)
    return jnp.where(rid == n - 1, rowl, rolled)


def _row_spec(w, col=0):
    return pl.BlockSpec((SEG, w), lambda i: (i, col))


def _halo_specs(w, m_rows):
    per = SEG // SUBLANES
    last = m_rows // SUBLANES - 1
    return (pl.BlockSpec((SUBLANES, w), lambda i: (jnp.maximum(i * per - 1, 0), 0)),
            pl.BlockSpec((SUBLANES, w), lambda i: (jnp.minimum((i + 1) * per, last), 0)))


def _const_spec(shape):
    nd = len(shape)
    return pl.BlockSpec(shape, lambda i: (0,) * nd)


def _segvec_spec(k, w):
    return pl.BlockSpec((1, k, w), lambda i: (i, 0, 0))


def _mm_kernel(x_ref, w_ref, o_ref, *, act):
    acc = _dot(_bf(x_ref[...]), w_ref[...])
    if act == "relu2":
        acc = jnp.square(jnp.maximum(acc, 0.0))
    o_ref[...] = acc.astype(o_ref.dtype)


def _mm(x, w, *, act=None, out_dtype=F32, tm=512, tn=512):
    m, k = x.shape
    n = w.shape[1]
    tn = min(tn, n)
    assert m % tm == 0 and n % tn == 0
    return pl.pallas_call(
        functools.partial(_mm_kernel, act=act),
        grid=(n // tn, m // tm),
        in_specs=[pl.BlockSpec((tm, k), lambda j, i: (i, 0)),
                  pl.BlockSpec((k, tn), lambda j, i: (0, j))],
        out_specs=pl.BlockSpec((tm, tn), lambda j, i: (i, j)),
        out_shape=jax.ShapeDtypeStruct((m, n), out_dtype),
        compiler_params=_cparams("parallel", "parallel"),
    )(x, w)


def _lowrank_kernel(x_ref, a_ref, b_ref, o_ref, *, act):
    t = _dot(x_ref[...], a_ref[...])
    if act == "tanh":
        t = jnp.tanh(t)
    elif act == "sigmoid":
        t = _sigmoid(t)
    o_ref[...] = _dot(_bf(t), b_ref[...])


def _lowrank(x, a, b, *, act=None, tm=512):
    m, k = x.shape
    r = a.shape[1]
    n = b.shape[1]
    return pl.pallas_call(
        functools.partial(_lowrank_kernel, act=act),
        grid=(m // tm,),
        in_specs=[pl.BlockSpec((tm, k), lambda i: (i, 0)),
                  pl.BlockSpec((k, r), lambda i: (0, 0)),
                  pl.BlockSpec((r, n), lambda i: (0, 0))],
        out_specs=pl.BlockSpec((tm, n), lambda i: (i, 0)),
        out_shape=jax.ShapeDtypeStruct((m, n), F32),
        compiler_params=_cparams("parallel"),
    )(x, a, b)


def _mod_kernel(c_ref, w_ref, b_ref, o_ref):
    c = c_ref[...]
    s = c * _sigmoid(c)
    o_ref[0] = _dot(_bf(s), _bf(w_ref[0])) + b_ref[0]


def _modulation(cond, w_mod, b_mod, tn=1536):
    depth, d, n = w_mod.shape
    return pl.pallas_call(
        _mod_kernel,
        grid=(depth, n // tn),
        in_specs=[pl.BlockSpec((SUBLANES, d), lambda l, j: (0, 0)),
                  pl.BlockSpec((1, d, tn), lambda l, j: (l, 0, j)),
                  pl.BlockSpec((1, 1, tn), lambda l, j: (l, 0, j))],
        out_specs=pl.BlockSpec((1, SUBLANES, tn), lambda l, j: (l, 0, j)),
        out_shape=jax.ShapeDtypeStruct((depth, SUBLANES, n), F32),
        compiler_params=_cparams("parallel", "parallel"),
    )(cond, w_mod, b_mod.reshape(depth, 1, n))


def _premod_kernel(x_ref, m_ref, h_ref):
    h = x_ref[...] * (1.0 + m_ref[0, 1:2, :]) + m_ref[0, 0:1, :]
    h_ref[...] = h.astype(h_ref.dtype)


def _premod(x, modseg, out_dtype):
    m, d = x.shape
    return pl.pallas_call(
        _premod_kernel,
        grid=(m // SEG,),
        in_specs=[_row_spec(d), _segvec_spec(N_MOD, d)],
        out_specs=_row_spec(d),
        out_shape=jax.ShapeDtypeStruct((m, d), out_dtype),
        compiler_params=_cparams("parallel"),
    )(x, modseg)


def _ln_kernel(x_ref, d_ref, m_ref, mn_ref, g_ref, b_ref, xo_ref, *rest, alpha, gate_idx, sh_idx):
    y = alpha * x_ref[...] + m_ref[0, gate_idx:gate_idx + 1, :] * d_ref[...]
    mu = jnp.mean(y, axis=-1, keepdims=True)
    yc = y - mu
    var = jnp.mean(yc * yc, axis=-1, keepdims=True)
    xn = yc * lax.rsqrt(var + LN_EPS) * g_ref[...] + b_ref[...]
    xo_ref[...] = xn
    if rest:
        h = xn * (1.0 + mn_ref[0, sh_idx + 1:sh_idx + 2, :]) + mn_ref[0, sh_idx:sh_idx + 1, :]
        rest[0][...] = h.astype(rest[0].dtype)


def _deepnorm_ln(x, delta, modseg, modseg_next, g, b, *, alpha, gate_idx, sh_idx, h_dtype):
    m, d = x.shape
    out_shape = [jax.ShapeDtypeStruct((m, d), F32)]
    out_specs = [_row_spec(d)]
    if h_dtype is not None:
        out_shape.append(jax.ShapeDtypeStruct((m, d), h_dtype))
        out_specs.append(_row_spec(d))
    res = pl.pallas_call(
        functools.partial(_ln_kernel, alpha=alpha, gate_idx=gate_idx, sh_idx=sh_idx),
        grid=(m // SEG,),
        in_specs=[_row_spec(d), _row_spec(d), _segvec_spec(N_MOD, d), _segvec_spec(N_MOD, d),
                  _const_spec((1, d)), _const_spec((1, d))],
        out_specs=out_specs,
        out_shape=out_shape,
        compiler_params=_cparams("parallel"),
    )(x, delta, modseg, modseg_next, g.reshape(1, d), b.reshape(1, d))
    return (res[0], res[1]) if h_dtype is not None else (res[0], None)


def _merge_masks(c):
    ti = lax.broadcasted_iota(jnp.int32, (c, c), 0)
    si = lax.broadcasted_iota(jnp.int32, (c, c), 1)
    masks = []
    b = 1
    while b < c:
        masks.append(jnp.logical_and(ti // (2 * b) == si // (2 * b), ti // b != si // b))
        b *= 2
    return masks


def _tri_inverse(lows, masks, eye):
    ts = [eye - jnp.where(masks[0], low, 0.0) for low in lows]
    for mk in masks[1:]:
        tb = [_bf(t) for t in ts]
        xs = [_bf(_dot(_bf(jnp.where(mk, low, 0.0)), b)) for low, b in zip(lows, tb)]
        ts = [t - _dot(b, x) for t, b, x in zip(ts, tb, xs)]
    return ts


def _scan_masks(e):
    ti = lax.broadcasted_iota(jnp.int32, (CH, CH), 0)
    si = lax.broadcasted_iota(jnp.int32, (CH, CH), 1)
    ahead = (ti - si) * (1 - 2 * e)
    incl = ahead >= 0
    strict = ahead > 0
    eye = (ti == si).astype(F32)
    return incl, strict, eye


def _chunk_row(base, nc):
    def index(e, b, n):
        return base + b * nc + n + e * (nc - 1 - 2 * n)
    return index


def _rw_pre_kernel(h_ref, hp_ref, hn_ref, mu_ref, *outs, bp, parts):
    is_start, is_end = _seg_flags(pl.program_id(0), bp, parts)
    h = h_ref[...]
    xx = 0.5 * (_shift_prev(h, hp_ref[...], is_start) + _shift_next(h, hn_ref[...], is_end)) - h
    for j, o_ref in enumerate(outs):
        o_ref[...] = _bf(h + xx * mu_ref[j:j + 1, :])


def _rw_pre(h, mu, bp, parts):
    m, d = h.shape
    hp, hn = _halo_specs(d, m)
    return pl.pallas_call(
        functools.partial(_rw_pre_kernel, bp=bp, parts=parts),
        grid=(m // SEG,),
        in_specs=[_row_spec(d), hp, hn, _const_spec((6, d))],
        out_specs=[_row_spec(d)] * 6,
        out_shape=[jax.ShapeDtypeStruct((m, d), BF16)] * 6,
        compiler_params=_cparams("parallel"),
    )(h, h, h, mu)


def _rw_mid_kernel(r_ref, k_ref, v_ref, wl_ref, al_ref, w0_ref, a0_ref, kk_ref, ka_ref, rk_ref,
                   ld_ref, be_ref, kd_ref, alpha_ref, bonus_ref):
    d = r_ref.shape[1]
    r = r_ref[...]
    k = k_ref[...]
    kkn = k * kk_ref[...]
    kk = kkn * lax.rsqrt(_segsum(kkn * kkn, RW_HEAD) + 1e-6)
    alpha_ref[...] = -kk
    bon = jnp.zeros_like(r)
    for e in range(2):
        cols = slice(e * d, (e + 1) * d)
        w_log = -_softplus(-(w0_ref[e:e + 1, :] + wl_ref[:, cols])) - 0.5
        ld_ref[:, cols] = -jnp.exp(w_log)
        a = _sigmoid(a0_ref[e:e + 1, :] + al_ref[:, cols])
        be_ref[:, cols] = kk * a
        kd = k * (1.0 + (a - 1.0) * ka_ref[...])
        kd_ref[:, cols] = kd
        bon = bon + r * kd * rk_ref[...]
    bonus_ref[...] = _segsum(bon, RW_HEAD) * v_ref[...]


def _rw_mid(r, k, v, wl, al, w0, a0, kk, ka, rk):
    m, d = r.shape
    return pl.pallas_call(
        _rw_mid_kernel,
        grid=(m // SEG,),
        in_specs=[_row_spec(d)] * 3 + [_row_spec(2 * d)] * 2
        + [_const_spec((2, d))] * 2 + [_const_spec((1, d))] * 3,
        out_specs=[_row_spec(2 * d)] * 3 + [_row_spec(d)] * 2,
        out_shape=[jax.ShapeDtypeStruct((m, 2 * d), F32)] * 3 + [jax.ShapeDtypeStruct((m, d), F32)] * 2,
        compiler_params=_cparams("parallel"),
    )(r, k, v, wl, al, w0, a0, kk.reshape(1, d), ka.reshape(1, d), rk.reshape(1, d))


def _rw_scan_kernel(r_ref, ld_ref, al_ref, be_ref, kd_ref, v_ref, s0_ref, o_ref, sT_ref, s_scr, *, nh, hd):
    e = pl.program_id(0)
    n = pl.program_id(2)

    @pl.when(n == 0)
    def _():
        s_scr[...] = s0_ref[0, 0]

    incl, strict, eye = _scan_masks(e)
    masks = _merge_masks(CH)
    ld = ld_ref[...]
    cs = _dot_mask_exact(incl.astype(BF16), ld)
    tot = jnp.sum(ld, axis=0, keepdims=True)
    g_end = jnp.exp(tot - cs)
    g_inv = jnp.exp(-cs)
    g_tot = jnp.exp(tot)
    be = be_ref[...]
    kd = kd_ref[...]
    ab = _bf(al_ref[...] * jnp.exp(cs - ld))
    qb = _bf(r_ref[...] * jnp.exp(cs))
    bt = _bf(be * g_inv)
    kt = _bf(kd * g_inv)
    b_end = _bf(be * g_end)
    k_end = _bf(kd * g_end)
    vb = _bf(v_ref[...])
    hs = range(nh)
    sls = [slice(h * hd, (h + 1) * hd) for h in hs]
    s_old = [s_scr[h] for h in hs]
    sb = [_bf(s) for s in s_old]
    aq = [jnp.concatenate([ab[:, sl], qb[:, sl]], axis=0) for sl in sls]
    xb = [_dot_nt(aq[h], bt[:, sls[h]]) for h in hs]
    xk = [_dot_nt(aq[h], kt[:, sls[h]]) for h in hs]
    t_inv = _tri_inverse([jnp.where(strict, -x[:CH], 0.0) for x in xb], masks, eye)
    a_ak = [_bf(jnp.where(strict, x[:CH], 0.0)) for x in xk]
    rhs = [_dot_nt(ab[:, sls[h]], sb[h]) + _dot(a_ak[h], vb[:, sls[h]]) for h in hs]
    u = [_bf(_dot(_bf(t_inv[h]), _bf(rhs[h]))) for h in hs]
    for h in hs:
        sl = sls[h]
        a_qb = _bf(jnp.where(incl, xb[h][CH:], 0.0))
        a_qk = _bf(jnp.where(incl, xk[h][CH:], 0.0))
        o_ref[:, sl] = _dot_nt(qb[:, sl], sb[h]) + _dot(a_qb, u[h]) + _dot(a_qk, vb[:, sl])
    for h in hs:
        sl = sls[h]
        s_scr[h] = s_old[h] * g_tot[:, sl] + _dot_tn(u[h], b_end[:, sl]) + _dot_tn(vb[:, sl], k_end[:, sl])

    @pl.when(n == pl.num_programs(2) - 1)
    def _():
        sT_ref[0, 0] = s_scr[...]


def _rw_scan(r, ld, alpha, be, kd, v, s0, *, base, nc):
    m, d = r.shape
    _, b, nh, hd, _ = s0.shape
    row = _chunk_row(base, nc)
    out_row = _chunk_row(0, nc)
    shared = pl.BlockSpec((CH, d), lambda e, bb, n: (row(e, bb, n), 0))
    perdir = pl.BlockSpec((CH, d), lambda e, bb, n: (row(e, bb, n), e))
    out = pl.BlockSpec((CH, d), lambda e, bb, n: (out_row(e, bb, n), e))
    st = pl.BlockSpec((1, 1, nh, hd, hd), lambda e, bb, n: (e, bb, 0, 0, 0))
    return pl.pallas_call(
        functools.partial(_rw_scan_kernel, nh=nh, hd=hd),
        grid=(2, b, nc),
        in_specs=[shared, perdir, shared, perdir, perdir, shared, st],
        out_specs=[out, st],
        out_shape=[jax.ShapeDtypeStruct((b * nc * CH, 2 * d), F32), jax.ShapeDtypeStruct(s0.shape, F32)],
        scratch_shapes=[pltpu.VMEM((nh, hd, hd), F32)],
        compiler_params=_cparams("parallel", "parallel", "arbitrary"),
    )(r, ld, alpha, be, kd, v, s0)


def _rw_post_kernel(o_ref, bonus_ref, g_ref, gng_ref, gnb_ref, y_ref):
    d = y_ref.shape[1]
    o = o_ref[:, :d] + o_ref[:, d:]
    oc = o - _segsum(o, RW_HEAD) * (1.0 / RW_HEAD)
    var = _segsum(oc * oc, RW_HEAD) * (1.0 / RW_HEAD)
    y = oc * lax.rsqrt(var + RW_GN_EPS) * gng_ref[...] + gnb_ref[...]
    y_ref[...] = _bf((y + bonus_ref[...]) * g_ref[...])


def _rw_post(o, bonus, g, gn_g, gn_b):
    m, d = bonus.shape
    return pl.pallas_call(
        _rw_post_kernel,
        grid=(m // SEG,),
        in_specs=[_row_spec(2 * d), _row_spec(d), _row_spec(d), _const_spec((1, d)), _const_spec((1, d))],
        out_specs=_row_spec(d),
        out_shape=jax.ShapeDtypeStruct((m, d), BF16),
        compiler_params=_cparams("parallel"),
    )(o, bonus, g, gn_g.reshape(1, d), gn_b.reshape(1, d))


def _rwkv_mixer(h, state, p, dims):
    bp, bs, parts = dims
    m, d = h.shape
    xr, xw, xk, xv, xa, xg = _rw_pre(h, p["mu"], bp, parts)
    r = _mm(xr, p["wr"])
    k = _mm(xk, p["wk"])
    v = _mm(xv, p["wv"])
    wl = _lowrank(xw, p["w1"], p["w2"], act="tanh")
    al = _lowrank(xa, p["a1"], p["a2"])
    g = _lowrank(xg, p["g1"], p["g2"], act="sigmoid")
    ld, be, kd, alpha, bonus = _rw_mid(r, k, v, wl, al, p["w0"], p["a0"], p["kk"], p["ka"], p["rk"])
    nh = d // RW_HEAD
    ncp = SEG // CH
    zero = jnp.zeros((2, bp, nh, RW_HEAD, RW_HEAD), F32)
    o_p, s_p = _rw_scan(r, ld, alpha, be, kd, v, zero, base=0, nc=ncp)
    s0 = jnp.swapaxes(state, 0, 1)
    o_s, _ = _rw_scan(r, ld, alpha, be, kd, v, s0, base=bp * ncp, nc=parts * ncp)
    o = jnp.concatenate([o_p, o_s], axis=0)
    y = _rw_post(o, bonus, g, p["gn_g"], p["gn_b"])
    return _mm(y, p["wo"]), jnp.swapaxes(s_p, 0, 1)


def _rope(y, cos, sin):
    w = y.shape[1]
    up = pltpu.roll(y, w - AT_HEAD // 4, 1)
    dn = pltpu.roll(y, AT_HEAD // 4, 1)
    lane = lax.broadcasted_iota(jnp.int32, y.shape, 1)
    first = (lane % (AT_HEAD // 2)) < (AT_HEAD // 4)
    return y * cos + jnp.where(first, up, dn) * sin


def _at_pre_kernel(qkv_ref, cos_ref, sin_ref, qg_ref, kg_ref, q_ref, kr_ref, kn_ref, *, dq, dk):
    q = qkv_ref[:, :dq]
    k = qkv_ref[:, dq:dq + dk]
    qn = q * lax.rsqrt(_segsum(q * q, AT_HEAD) * (1.0 / AT_HEAD) + 1e-6) * qg_ref[...]
    kn = k * lax.rsqrt(_segsum(k * k, AT_HEAD) * (1.0 / AT_HEAD) + 1e-6) * kg_ref[...]
    kn_ref[...] = kn
    q_ref[...] = _bf(_rope(qn, cos_ref[...], sin_ref[...]))
    kr_ref[...] = _bf(_rope(kn, cos_ref[:, :dk], sin_ref[:, :dk]))


def _at_pre(qkv, cos_t, sin_t, qg, kg, bp, parts, dq, dk):
    m, w = qkv.shape
    tab = pl.BlockSpec((SEG, dq), lambda i: (jnp.where(i < bp, 0, 1 + (i - bp) % parts), 0))
    return pl.pallas_call(
        functools.partial(_at_pre_kernel, dq=dq, dk=dk),
        grid=(m // SEG,),
        in_specs=[_row_spec(w), tab, tab, _const_spec((1, dq)), _const_spec((1, dk))],
        out_specs=[_row_spec(dq), _row_spec(dk), _row_spec(dk)],
        out_shape=[jax.ShapeDtypeStruct((m, dq), BF16), jax.ShapeDtypeStruct((m, dk), BF16),
                   jax.ShapeDtypeStruct((m, dk), F32)],
        compiler_params=_cparams("parallel"),
    )(qkv, cos_t, sin_t, qg, kg)


def _attn_kernel(q_ref, k_ref, v_ref, o_ref, *, scale):
    nkv = k_ref.shape[2] // AT_HEAD
    for g in range(nkv):
        kg = k_ref[0, :, g * AT_HEAD:(g + 1) * AT_HEAD]
        vg = v_ref[0, :, g * AT_HEAD:(g + 1) * AT_HEAD]
        for j in range(AT_GROUP):
            sl = slice((g * AT_GROUP + j) * AT_HEAD, (g * AT_GROUP + j + 1) * AT_HEAD)
            s = _dot_nt(q_ref[:, sl], kg) * scale
            p = jnp.exp(s - jnp.max(s, axis=-1, keepdims=True))
            l = jnp.sum(p, axis=-1, keepdims=True)
            o_ref[:, sl] = (_dot(_bf(p), vg) / l).astype(o_ref.dtype)


def _attention(q, k, v, *, base, nq):
    dq = q.shape[1]
    b, tk, dk = k.shape
    kv = pl.BlockSpec((1, tk, dk), lambda bb, j: (bb, 0, 0))
    return pl.pallas_call(
        functools.partial(_attn_kernel, scale=AT_HEAD ** -0.5),
        grid=(b, nq),
        in_specs=[pl.BlockSpec((SEG, dq), lambda bb, j: (base + bb * nq + j, 0)), kv, kv],
        out_specs=pl.BlockSpec((SEG, dq), lambda bb, j: (bb * nq + j, 0)),
        out_shape=jax.ShapeDtypeStruct((b * nq * SEG, dq), BF16),
        compiler_params=_cparams("parallel", "parallel"),
    )(q, k, v)


def _rope_tables(t, heads):
    rows = t // GRID_W
    row = jnp.repeat(jnp.arange(rows), GRID_W)
    col = jnp.tile(jnp.arange(GRID_W), rows)
    n_freq = AT_HEAD // 4
    inv_freq = ROPE_THETA ** (-jnp.arange(n_freq, dtype=F32) / n_freq)
    ang = jnp.stack([row, col], axis=-1).astype(F32)[:, :, None] * inv_freq
    cos = jnp.repeat(jnp.cos(ang)[:, :, None, :], 2, axis=2).reshape(t, AT_HEAD)
    sin = jnp.sin(ang)
    sin = jnp.stack([-sin, sin], axis=2).reshape(t, AT_HEAD)
    ident = (jnp.ones((SEG, AT_HEAD), F32), jnp.zeros((SEG, AT_HEAD), F32))
    cos = jnp.tile(jnp.concatenate([ident[0], cos], axis=0), (1, heads))
    sin = jnp.tile(jnp.concatenate([ident[1], sin], axis=0), (1, heads))
    return cos, sin


def _attention_mixer(h, cache_k, cache_v, p, dims):
    bp, bs, parts = dims
    m, d = h.shape
    dk = cache_k.shape[-1] * cache_k.shape[-2]
    heads = d // AT_HEAD
    qkv = _mm(h, p["wqkv"])
    cos_t, sin_t = _rope_tables(parts * SEG, heads)
    q, kr, kn = _at_pre(qkv, cos_t, sin_t, jnp.tile(p["qn"], heads).reshape(1, d),
                        jnp.tile(p["kn"], dk // AT_HEAD).reshape(1, dk), bp, parts, d, dk)
    v = qkv[:, d + dk:]
    np_ = bp * SEG
    o_p = _attention(q, kr[:np_].reshape(bp, SEG, dk), _bf(v[:np_]).reshape(bp, SEG, dk), base=0, nq=1)
    k_s = jnp.concatenate([_bf(cache_k.reshape(bs, -1, dk)), kr[np_:].reshape(bs, parts * SEG, dk)], axis=1)
    v_s = jnp.concatenate([_bf(cache_v.reshape(bs, -1, dk)), _bf(v[np_:]).reshape(bs, parts * SEG, dk)], axis=1)
    o_s = _attention(q, k_s, v_s, base=bp, nq=parts)
    delta = _mm(jnp.concatenate([o_p, o_s], axis=0), p["wo"])
    return delta, kn[:np_], v[:np_]


def _dn_pre_kernel(x_ref, xp_ref, xn_ref, gt_ref, cw_ref, ga_ref, gb_ref, q_ref, k_ref, v_ref, gd_ref,
                   *, bp, parts, d, dk):
    is_start, is_end = _seg_flags(pl.program_id(0), bp, parts)
    x = x_ref[...]
    c = (cw_ref[0:1, :] * _shift_prev(x, xp_ref[...], is_start) + cw_ref[1:2, :] * x
         + cw_ref[2:3, :] * _shift_next(x, xn_ref[...], is_end))
    s = c * _sigmoid(c)
    q = s[:, :d]
    k = s[:, d:2 * d]
    q_ref[...] = q * lax.rsqrt(_segsum(q * q, dk) + 1e-6) * (dk ** -0.5)
    k_ref[...] = k * lax.rsqrt(_segsum(k * k, dk) + 1e-6)
    v_ref[...] = s[:, 2 * d:]
    gt = gt_ref[...]
    lane = lax.broadcasted_iota(jnp.int32, gt.shape, 1) % LANES
    gd_ref[...] = jnp.where(lane < DN_HEADS, -jnp.exp(ga_ref[...]) * _softplus(gt + gb_ref[...]), _sigmoid(gt))


def _dn_pre(proj, gates, conv_w, ga, gb, bp, parts, d, dk):
    m = proj.shape[0]
    w3 = 3 * d
    hp, hn = _halo_specs(w3, m)
    return pl.pallas_call(
        functools.partial(_dn_pre_kernel, bp=bp, parts=parts, d=d, dk=dk),
        grid=(m // SEG,),
        in_specs=[_row_spec(w3), hp, hn, _row_spec(2 * LANES), _const_spec((3, w3)),
                  _const_spec((1, 2 * LANES)), _const_spec((1, 2 * LANES))],
        out_specs=[_row_spec(d)] * 3 + [_row_spec(2 * LANES)],
        out_shape=[jax.ShapeDtypeStruct((m, d), F32)] * 3 + [jax.ShapeDtypeStruct((m, 2 * LANES), F32)],
        compiler_params=_cparams("parallel"),
    )(proj, proj, proj, gates, conv_w, ga, gb)


def _dn_scan_kernel(q_ref, k_ref, v_ref, gd_ref, s0_ref, o_ref, sT_ref, s_scr, *, nh, hd):
    e = pl.program_id(0)
    n = pl.program_id(2)

    @pl.when(n == 0)
    def _():
        s_scr[...] = s0_ref[0, 0]

    incl, strict, eye = _scan_masks(e)
    masks = _merge_masks(CH)
    ones = jnp.ones((CH, CH), BF16)
    gd = gd_ref[...]
    cs = _dot_mask_exact(incl.astype(BF16), gd)
    tot = jnp.sum(gd, axis=0, keepdims=True)
    e_cs = jnp.exp(cs)
    e_end = jnp.exp(tot - cs)
    e_tot = jnp.exp(tot)
    hs = range(nh)
    sls = [slice(h * hd, (h + 1) * hd) for h in hs]
    g_col = [jnp.broadcast_to(cs[:, h:h + 1], (CH, CH)) for h in hs]
    g_row = [_dot_mask_exact(ones, jnp.where(eye > 0, g, 0.0)) for g in g_col]
    dec = [jnp.where(incl, jnp.exp(jnp.where(incl, g_col[h] - g_row[h], 0.0)), 0.0) for h in hs]
    beta = [gd[:, nh + h:nh + h + 1] for h in hs]
    kh = [k_ref[:, sl] for sl in sls]
    kb = [kh[h] * beta[h] for h in hs]
    khb = [_bf(x) for x in kh]
    low = [jnp.where(strict, _dot_nt(_bf(kb[h]), khb[h]) * dec[h], 0.0) for h in hs]
    t_inv = [_bf(t) for t in _tri_inverse(low, masks, eye)]
    u = [_dot(t_inv[h], _bf(v_ref[:, sls[h]] * beta[h])) for h in hs]
    w = [_bf(_dot(t_inv[h], _bf(kb[h] * e_cs[:, h:h + 1]))) for h in hs]
    s_old = [s_scr[h] for h in hs]
    sb = [_bf(s) for s in s_old]
    v_new = [_bf(u[h] - _dot(w[h], sb[h])) for h in hs]
    for h in hs:
        qh = q_ref[:, sls[h]]
        qk = _bf(jnp.where(incl, _dot_nt(_bf(qh), khb[h]) * dec[h], 0.0))
        o_ref[:, sls[h]] = _dot(_bf(qh * e_cs[:, h:h + 1]), sb[h]) + _dot(qk, v_new[h])
    for h in hs:
        s_scr[h] = (s_old[h] * e_tot[:, h:h + 1]
                    + _dot_tn(_bf(kh[h] * e_end[:, h:h + 1]), v_new[h]))

    @pl.when(n == pl.num_programs(2) - 1)
    def _():
        sT_ref[0, 0] = s_scr[...]


def _dn_scan(q, k, v, gd, s0, *, base, nc):
    m, d = q.shape
    _, b, nh, hd, _ = s0.shape
    row = _chunk_row(base, nc)
    out_row = _chunk_row(0, nc)
    shared = pl.BlockSpec((CH, d), lambda e, bb, n: (row(e, bb, n), 0))
    gates = pl.BlockSpec((CH, LANES), lambda e, bb, n: (row(e, bb, n), e))
    out = pl.BlockSpec((CH, d), lambda e, bb, n: (out_row(e, bb, n), e))
    st = pl.BlockSpec((1, 1, nh, hd, hd), lambda e, bb, n: (e, bb, 0, 0, 0))
    return pl.pallas_call(
        functools.partial(_dn_scan_kernel, nh=nh, hd=hd),
        grid=(2, b, nc),
        in_specs=[shared, shared, shared, gates, st],
        out_specs=[out, st],
        out_shape=[jax.ShapeDtypeStruct((b * nc * CH, 2 * d), F32), jax.ShapeDtypeStruct(s0.shape, F32)],
        scratch_shapes=[pltpu.VMEM((nh, hd, hd), F32)],
        compiler_params=_cparams("parallel", "parallel", "arbitrary"),
    )(q, k, v, gd, s0)


def _dn_post_kernel(o_ref, z_ref, ng_ref, y_ref, *, dv):
    d = y_ref.shape[1]
    o = o_ref[:, :d] + o_ref[:, d:]
    z = z_ref[...]
    y = o * lax.rsqrt(_segsum(o * o, dv) * (1.0 / dv) + 1e-6) * ng_ref[...]
    y_ref[...] = _bf(y * (z * _sigmoid(z)))


def _dn_post(o, proj, ng, d, dv):
    m = o.shape[0]
    return pl.pallas_call(
        functools.partial(_dn_post_kernel, dv=dv),
        grid=(m // SEG,),
        in_specs=[_row_spec(2 * d), _row_spec(d, col=3), _const_spec((1, d))],
        out_specs=_row_spec(d),
        out_shape=jax.ShapeDtypeStruct((m, d), BF16),
        compiler_params=_cparams("parallel"),
    )(o, proj, ng)


def _deltanet_mixer(h, state, p, dims):
    bp, bs, parts = dims
    m, d = h.shape
    dk = d // DN_HEADS
    proj = _mm(h, p["win"])
    gates = _mm(h, p["wg"], tn=2 * LANES)
    q, k, v, gd = _dn_pre(proj, gates, p["conv"], p["ga"], p["gb"], bp, parts, d, dk)
    ncp = SEG // CH
    zero = jnp.zeros((2, bp, DN_HEADS, dk, dk), F32)
    o_p, s_p = _dn_scan(q, k, v, gd, zero, base=0, nc=ncp)
    o_s, _ = _dn_scan(q, k, v, gd, jnp.swapaxes(state, 0, 1), base=bp * ncp, nc=parts * ncp)
    o = jnp.concatenate([o_p, o_s], axis=0)
    y = _dn_post(o, proj, jnp.tile(p["ng"], DN_HEADS).reshape(1, d), d, dk)
    return _mm(y, p["wo"]), jnp.swapaxes(s_p, 0, 1)


def _gate_columns(w_gate):
    dd = w_gate.shape[0]
    pad = jnp.zeros((dd, LANES - 2 * DN_HEADS), w_gate.dtype)
    cols = []
    for e in range(2):
        cols += [w_gate[:, e * DN_HEADS:(e + 1) * DN_HEADS],
                 w_gate[:, (2 + e) * DN_HEADS:(3 + e) * DN_HEADS], pad]
    return jnp.concatenate(cols, axis=1)


def _gate_lanes(x):
    pad = jnp.zeros((LANES - DN_HEADS,), x.dtype)
    return jnp.concatenate([x[0], pad, x[1], pad]).reshape(1, 2 * LANES)


def kernel(x_prompt, x_sample, state_rwkv, cache_k, cache_v, state_delta, c, c_ctx, w_mod, b_mod, ln_g, ln_b, w_fc1, w_fc2, rw_mu, rw_wrkv, rw_w0, rw_w1, rw_w2, rw_a0, rw_a1, rw_a2, rw_g1, rw_g2, rw_kk, rw_ka, rw_rk, rw_gn_g, rw_gn_b, rw_wo, at_wqkv, at_qn, at_kn, at_wo, dn_win, dn_conv, dn_alog, dn_dtb, dn_ng, dn_wo):
    bp, tp, d = x_prompt.shape
    bs, ts, _ = x_sample.shape
    assert tp == SEG and ts % SEG == 0 and bs + 1 <= SUBLANES
    parts = ts // SEG
    dims = (bp, bs, parts)
    depth = w_mod.shape[0]
    alpha = (2.0 * depth) ** 0.25

    x = jnp.concatenate([x_prompt.reshape(bp * tp, d), x_sample.reshape(bs * ts, d)], axis=0)
    cond = jnp.concatenate([c, c_ctx[None, :], jnp.zeros((SUBLANES - bs - 1, d), F32)], axis=0)
    mods = _modulation(cond, w_mod, b_mod)
    seg_row = jnp.array([bs] * bp + [b for b in range(bs) for _ in range(parts)], jnp.int32)
    modseg = [mods[l][seg_row].reshape(bp + bs * parts, N_MOD, d) for l in range(depth)]

    def mixer_in_dtype(l):
        return F32 if l % 3 == 0 else BF16

    new_rwkv, new_k, new_v, new_delta = [], [], [], []
    h = _premod(x, modseg[0], mixer_in_dtype(0))
    for l in range(depth):
        kind, j = l % 3, l // 3
        if kind == 0:
            z64 = jnp.zeros((RW_HEAD, d), F32)
            prm = dict(
                mu=rw_mu[j], wr=_bf(rw_wrkv[j, 0]), wk=_bf(rw_wrkv[j, 1]), wv=_bf(rw_wrkv[j, 2]),
                w1=_bf(jnp.concatenate([rw_w1[j, 0], rw_w1[j, 1]], axis=1)),
                w2=_bf(jnp.concatenate([jnp.concatenate([rw_w2[j, 0], z64], axis=1),
                                        jnp.concatenate([z64, rw_w2[j, 1]], axis=1)], axis=0)),
                a1=_bf(jnp.concatenate([rw_a1[j, 0], rw_a1[j, 1]], axis=1)),
                a2=_bf(jnp.concatenate([jnp.concatenate([rw_a2[j, 0], z64], axis=1),
                                        jnp.concatenate([z64, rw_a2[j, 1]], axis=1)], axis=0)),
                g1=_bf(rw_g1[j]), g2=_bf(rw_g2[j]), w0=rw_w0[j], a0=rw_a0[j],
                kk=rw_kk[j], ka=rw_ka[j], rk=rw_rk[j].reshape(-1), gn_g=rw_gn_g[j], gn_b=rw_gn_b[j],
                wo=_bf(rw_wo[j]))
            delta, s_new = _rwkv_mixer(h, state_rwkv[:, j], prm, dims)
            new_rwkv.append(s_new)
        elif kind == 1:
            prm = dict(wqkv=_bf(at_wqkv[j]), qn=at_qn[j], kn=at_kn[j], wo=_bf(at_wo[j]))
            delta, kp, vp = _attention_mixer(h, cache_k[:, j], cache_v[:, j], prm, dims)
            new_k.append(kp.reshape(bp, tp, -1, AT_HEAD))
            new_v.append(vp.reshape(bp, tp, -1, AT_HEAD))
        else:
            prm = dict(win=_bf(dn_win[j, :, :4 * d]), wg=_bf(_gate_columns(dn_win[j, :, 4 * d:])),
                       conv=dn_conv[j], ga=_gate_lanes(dn_alog[j]), gb=_gate_lanes(dn_dtb[j]),
                       ng=dn_ng[j], wo=_bf(dn_wo[j]))
            delta, s_new = _deltanet_mixer(h, state_delta[:, j], prm, dims)
            new_delta.append(s_new)
        x, h = _deepnorm_ln(x, delta, modseg[l], modseg[l], ln_g[l, 0], ln_b[l, 0],
                            alpha=alpha, gate_idx=2, sh_idx=3, h_dtype=BF16)
        u = _mm(h, _bf(w_fc1[l]), act="relu2", out_dtype=BF16)
        delta = _mm(u, _bf(w_fc2[l]))
        nxt = min(l + 1, depth - 1)
        x, h = _deepnorm_ln(x, delta, modseg[l], modseg[nxt], ln_g[l, 1], ln_b[l, 1],
                            alpha=alpha, gate_idx=5, sh_idx=0,
                            h_dtype=mixer_in_dtype(l + 1) if l + 1 < depth else None)
    y_prompt = x[:bp * tp].reshape(bp, tp, d)
    y_sample = x[bp * tp:].reshape(bs, ts, d)
    return (y_prompt, y_sample, jnp.stack(new_rwkv, axis=1), jnp.stack(new_k, axis=1),
            jnp.stack(new_v, axis=1), jnp.stack(new_delta, axis=1))
```

```python
import functools

import jax
import jax.numpy as jnp
import numpy as np
from jax import lax
from jax.experimental import pallas as pl
from jax.experimental.pallas import tpu as pltpu

F32 = jnp.float32
BF16 = jnp.bfloat16

SEG = 256
CH = 64
LANES = 128
SUBLANES = 8
VMEM_LIMIT = 48 * 1024 * 1024

N_MOD = 6
LN_EPS = 1e-5
RW_HEAD = 64
RW_GN_EPS = 64e-5
AT_HEAD = 64
AT_GROUP = 4
GRID_W = 64
ROPE_THETA = 10000.0
DN_HEADS = 8


def _cparams(*sem):
    return pltpu.CompilerParams(dimension_semantics=sem, vmem_limit_bytes=VMEM_LIMIT)


def _dot(a, b):
    return jnp.dot(a, b, preferred_element_type=F32)


def _dot_nt(a, b):
    return lax.dot_general(a, b, (((1,), (1,)), ((), ())), preferred_element_type=F32)


def _dot_tn(a, b):
    return lax.dot_general(a, b, (((0,), (0,)), ((), ())), preferred_element_type=F32)


def _bf(x):
    return x.astype(BF16)


def _split3(x):
    hi = _bf(x)
    r1 = x - hi.astype(F32)
    mid = _bf(r1)
    lo = _bf(r1 - mid.astype(F32))
    return hi, mid, lo


def _dot_mask_exact(mask_bf, x):
    hi, mid, lo = _split3(x)
    return _dot(mask_bf, hi) + _dot(mask_bf, mid) + _dot(mask_bf, lo)


def _segsum(x, seg):
    w = x.shape[1]
    li = lax.broadcasted_iota(jnp.int32, (LANES, LANES), 0) // seg
    lj = lax.broadcasted_iota(jnp.int32, (LANES, LANES), 1) // seg
    bd = (li == lj).astype(BF16)
    outs = []
    for j in range(w // LANES):
        xs = x[:, LANES * j:LANES * (j + 1)]
        hi = _bf(xs)
        lo = _bf(xs - hi.astype(F32))
        outs.append(_dot(hi, bd) + _dot(lo, bd))
    return outs[0] if len(outs) == 1 else jnp.concatenate(outs, axis=1)


def _softplus(x):
    return jnp.maximum(x, 0.0) + jnp.log1p(jnp.exp(-jnp.abs(x)))


def _sigmoid(x):
    return 1.0 / (1.0 + jnp.exp(-x))


def _seg_flags(i, bp, parts):
    j = (i - bp) % parts
    is_ctx = i < bp
    return jnp.logical_or(is_ctx, j == 0), jnp.logical_or(is_ctx, j == parts - 1)


def _shift_prev(x, xp8, is_start):
    rolled = pltpu.roll(x, 1, 0)
    row0 = jnp.where(is_start, 0.0, xp8[SUBLANES - 1:SUBLANES, :])
    rid = lax.broadcasted_iota(jnp.int32, x.shape, 0)
    return jnp.where(rid == 0, row0, rolled)


def _shift_next(x, xn8, is_end):
    n = x.shape[0]
    rolled = pltpu.roll(x, n - 1, 0)
    rowl = jnp.where(is_end, 0.0, xn8[0:1, :])
    rid = lax.broadcasted_iota(jnp.int32, x.shape, 0)
    return jnp.where(rid == n - 1, rowl, rolled)


def _row_spec(w, col=0):
    return pl.BlockSpec((SEG, w), lambda i: (i, col))


def _halo_specs(w, m_rows):
    per = SEG // SUBLANES
    last = m_rows // SUBLANES - 1
    return (pl.BlockSpec((SUBLANES, w), lambda i: (jnp.maximum(i * per - 1, 0), 0)),
            pl.BlockSpec((SUBLANES, w), lambda i: (jnp.minimum((i + 1) * per, last), 0)))


def _const_spec(shape):
    nd = len(shape)
    return pl.BlockSpec(shape, lambda i: (0,) * nd)


def _segvec_spec(k, w):
    return pl.BlockSpec((1, k, w), lambda i: (i, 0, 0))


def _mm_kernel(x_ref, w_ref, o_ref, *, act):
    acc = _dot(_bf(x_ref[...]), w_ref[...])
    if act == "relu2":
        acc = jnp.square(jnp.maximum(acc, 0.0))
    o_ref[...] = acc.astype(o_ref.dtype)


def _mm(x, w, *, act=None, out_dtype=F32, tm=512, tn=512):
    m, k = x.shape
    n = w.shape[1]
    tn = min(tn, n)
    assert m % tm == 0 and n % tn == 0
    return pl.pallas_call(
        functools.partial(_mm_kernel, act=act),
        grid=(n // tn, m // tm),
        in_specs=[pl.BlockSpec((tm, k), lambda j, i: (i, 0)),
                  pl.BlockSpec((k, tn), lambda j, i: (0, j))],
        out_specs=pl.BlockSpec((tm, tn), lambda j, i: (i, j)),
        out_shape=jax.ShapeDtypeStruct((m, n), out_dtype),
        compiler_params=_cparams("parallel", "parallel"),
    )(x, w)


def _mod_kernel(c_ref, w_ref, b_ref, o_ref):
    c = c_ref[...]
    s = c * _sigmoid(c)
    o_ref[0] = _dot(_bf(s), _bf(w_ref[0])) + b_ref[0]


def _modulation(cond, w_mod, b_mod, tn=1536):
    depth, d, n = w_mod.shape
    return pl.pallas_call(
        _mod_kernel,
        grid=(depth, n // tn),
        in_specs=[pl.BlockSpec((SUBLANES, d), lambda l, j: (0, 0)),
                  pl.BlockSpec((1, d, tn), lambda l, j: (l, 0, j)),
                  pl.BlockSpec((1, 1, tn), lambda l, j: (l, 0, j))],
        out_specs=pl.BlockSpec((1, SUBLANES, tn), lambda l, j: (l, 0, j)),
        out_shape=jax.ShapeDtypeStruct((depth, SUBLANES, n), F32),
        compiler_params=_cparams("parallel", "parallel"),
    )(cond, w_mod, b_mod.reshape(depth, 1, n))


def _premod_kernel(x_ref, m_ref, h_ref):
    h = x_ref[...] * (1.0 + m_ref[0, 1:2, :]) + m_ref[0, 0:1, :]
    h_ref[...] = h.astype(h_ref.dtype)


def _premod(x, modseg, out_dtype):
    m, d = x.shape
    return pl.pallas_call(
        _premod_kernel,
        grid=(m // SEG,),
        in_specs=[_row_spec(d), _segvec_spec(N_MOD, d)],
        out_specs=_row_spec(d),
        out_shape=jax.ShapeDtypeStruct((m, d), out_dtype),
        compiler_params=_cparams("parallel"),
    )(x, modseg)


def _seg_rows(m_ref, idx, spt):
    d = m_ref.shape[2]
    rows = [jnp.broadcast_to(m_ref[s, idx:idx + 1, :], (SEG, d)) for s in range(spt)]
    return rows[0] if spt == 1 else jnp.concatenate(rows, axis=0)


def _ln_epilogue(x, delta, m_ref, mn_ref, g_ref, b_ref, xo_ref, ho_ref, *, alpha, gate_idx, sh_idx, spt):
    y = alpha * x + _seg_rows(m_ref, gate_idx, spt) * delta
    mu = jnp.mean(y, axis=-1, keepdims=True)
    yc = y - mu
    var = jnp.mean(yc * yc, axis=-1, keepdims=True)
    xn = yc * lax.rsqrt(var + LN_EPS) * g_ref[...] + b_ref[...]
    xo_ref[...] = xn
    if ho_ref is not None:
        h = xn * (1.0 + _seg_rows(mn_ref, sh_idx + 1, spt)) + _seg_rows(mn_ref, sh_idx, spt)
        ho_ref[...] = h.astype(ho_ref.dtype)


def _mm_ln_kernel(y_ref, w_ref, x_ref, m_ref, mn_ref, g_ref, b_ref, xo_ref, *rest, **kw):
    delta = _dot(y_ref[...], w_ref[...])
    _ln_epilogue(x_ref[...], delta, m_ref, mn_ref, g_ref, b_ref, xo_ref, rest[0] if rest else None, **kw)


def _mlp_ln_kernel(h_ref, w1_ref, w2_ref, x_ref, m_ref, mn_ref, g_ref, b_ref, xo_ref, *rest, fc, **kw):
    hb = h_ref[...]
    delta = jnp.zeros(x_ref.shape, F32)
    for j in range(w1_ref.shape[1] // fc):
        u = jnp.square(jnp.maximum(_dot(hb, w1_ref[:, j * fc:(j + 1) * fc]), 0.0))
        delta = delta + _dot(_bf(u), w2_ref[j * fc:(j + 1) * fc, :])
    _ln_epilogue(x_ref[...], delta, m_ref, mn_ref, g_ref, b_ref, xo_ref, rest[0] if rest else None, **kw)


def _resident(shape):
    nd = len(shape)
    return pl.BlockSpec(shape, lambda i: (0,) * nd, pipeline_mode=pl.Buffered(1))


def _sublayer_ln(kern, lead, weights, x, modseg, modseg_next, g, b, *, alpha, gate_idx, sh_idx, h_dtype,
                 **static):
    m, d = x.shape
    nseg = modseg.shape[0]
    spt = 2 if nseg % 2 == 0 else 1
    tm = spt * SEG
    row = lambda w: pl.BlockSpec((tm, w), lambda i: (i, 0))
    segv = pl.BlockSpec((spt, N_MOD, d), lambda i: (i, 0, 0))
    out_shape = [jax.ShapeDtypeStruct((m, d), F32)]
    out_specs = [row(d)]
    if h_dtype is not None:
        out_shape.append(jax.ShapeDtypeStruct((m, d), h_dtype))
        out_specs.append(row(d))
    res = pl.pallas_call(
        functools.partial(kern, alpha=alpha, gate_idx=gate_idx, sh_idx=sh_idx, spt=spt, **static),
        grid=(m // tm,),
        in_specs=[row(lead.shape[1])] + [_resident(w.shape) for w in weights]
        + [row(d), segv, segv, _const_spec((1, d)), _const_spec((1, d))],
        out_specs=out_specs,
        out_shape=out_shape,
        compiler_params=_cparams("parallel"),
    )(lead, *weights, x, modseg, modseg_next, g.reshape(1, d), b.reshape(1, d))
    return (res[0], res[1]) if h_dtype is not None else (res[0], None)


def _merge_masks(c):
    ti = lax.broadcasted_iota(jnp.int32, (c, c), 0)
    si = lax.broadcasted_iota(jnp.int32, (c, c), 1)
    masks = []
    b = 1
    while b < c:
        masks.append(jnp.logical_and(ti // (2 * b) == si // (2 * b), ti // b != si // b))
        b *= 2
    return masks


def _tri_inverse(lows, masks, eye):
    ts = [eye - jnp.where(masks[0], low, 0.0) for low in lows]
    for mk in masks[1:]:
        tb = [_bf(t) for t in ts]
        xs = [_bf(_dot(_bf(jnp.where(mk, low, 0.0)), b)) for low, b in zip(lows, tb)]
        ts = [t - _dot(b, x) for t, b, x in zip(ts, tb, xs)]
    return ts


def _scan_masks(e):
    ti = lax.broadcasted_iota(jnp.int32, (CH, CH), 0)
    si = lax.broadcasted_iota(jnp.int32, (CH, CH), 1)
    ahead = (ti - si) * (1 - 2 * e)
    incl = ahead >= 0
    strict = ahead > 0
    eye = (ti == si).astype(F32)
    return incl, strict, eye


def _scan_table(bp, bs, parts):
    ncp = SEG // CH
    ncs = parts * ncp
    steps = [(b * ncp, n, ncp, b) for b in range(bp) for n in range(ncp)]
    steps += [(bp * ncp + b * ncs, n, ncs, bp + b) for b in range(bs) for n in range(ncs)]
    return jnp.asarray(np.array(steps, np.int32).T)


def _scan_row(col):
    def index(e, t, tbl):
        n = tbl[1, t]
        return tbl[0, t] + n + e * (tbl[2, t] - 1 - 2 * n), (e if col is None else col)
    return index


def _scan_state(e, t, tbl):
    return e, tbl[3, t], 0, 0, 0


def _rw_proj_kernel(h_ref, hp_ref, hn_ref, mu_ref, wr_ref, wk_ref, wv_ref, w1_ref, w2_ref, a1_ref, a2_ref,
                    g1_ref, g2_ref, w0_ref, a0_ref, kk_ref, ka_ref, rk_ref,
                    r_ref, v_ref, g_ref, ld_ref, be_ref, kd_ref, alpha_ref, bonus_ref, *, bp, parts):
    d = h_ref.shape[1]
    is_start, is_end = _seg_flags(pl.program_id(0), bp, parts)
    h = h_ref[...]
    xx = 0.5 * (_shift_prev(h, hp_ref[...], is_start) + _shift_next(h, hn_ref[...], is_end)) - h

    def mix(j):
        return _bf(h + xx * mu_ref[j:j + 1, :])

    r = _dot(mix(0), wr_ref[...])
    k = _dot(mix(2), wk_ref[...])
    v = _dot(mix(3), wv_ref[...])
    wl = _dot(_bf(jnp.tanh(_dot(mix(1), w1_ref[...]))), w2_ref[...])
    al = _dot(_bf(_dot(mix(4), a1_ref[...])), a2_ref[...])
    g_ref[...] = _dot(_bf(_sigmoid(_dot(mix(5), g1_ref[...]))), g2_ref[...])
    r_ref[...] = r
    v_ref[...] = v
    kkn = k * kk_ref[...]
    kk = kkn * lax.rsqrt(_segsum(kkn * kkn, RW_HEAD) + 1e-6)
    alpha_ref[...] = -kk
    bon = jnp.zeros_like(r)
    for e in range(2):
        cols = slice(e * d, (e + 1) * d)
        w_log = -_softplus(-(w0_ref[e:e + 1, :] + wl[:, cols])) - 0.5
        ld_ref[:, cols] = -jnp.exp(w_log)
        a = _sigmoid(a0_ref[e:e + 1, :] + al[:, cols])
        be_ref[:, cols] = kk * a
        kd = k * (1.0 + (a - 1.0) * ka_ref[...])
        kd_ref[:, cols] = kd
        bon = bon + r * kd * rk_ref[...]
    bonus_ref[...] = _segsum(bon, RW_HEAD) * v


def _rw_proj(h, p, bp, parts):
    m, d = h.shape
    hp, hn = _halo_specs(d, m)
    weights = [p[n] for n in ("wr", "wk", "wv", "w1", "w2", "a1", "a2", "g1", "g2")]
    vecs = [p["w0"], p["a0"], p["kk"].reshape(1, d), p["ka"].reshape(1, d), p["rk"].reshape(1, d)]
    return pl.pallas_call(
        functools.partial(_rw_proj_kernel, bp=bp, parts=parts),
        grid=(m // SEG,),
        in_specs=[_row_spec(d), hp, hn, _const_spec((6, d))] + [_resident(w.shape) for w in weights]
        + [_const_spec(x.shape) for x in vecs],
        out_specs=[_row_spec(d)] * 3 + [_row_spec(2 * d)] * 3 + [_row_spec(d)] * 2,
        out_shape=[jax.ShapeDtypeStruct((m, d), F32)] * 3 + [jax.ShapeDtypeStruct((m, 2 * d), F32)] * 3
        + [jax.ShapeDtypeStruct((m, d), F32)] * 2,
        compiler_params=_cparams("parallel"),
    )(h, h, h, p["mu"], *weights, *vecs)


def _rw_scan_kernel(tbl_ref, r_ref, ld_ref, al_ref, be_ref, kd_ref, v_ref, s0_ref, o_ref, sT_ref, s_scr,
                    *, nh, hd):
    e = pl.program_id(0)
    t = pl.program_id(1)
    n = tbl_ref[1, t]

    @pl.when(n == 0)
    def _():
        s_scr[...] = s0_ref[0, 0]

    incl, strict, eye = _scan_masks(e)
    masks = _merge_masks(CH)
    ld = ld_ref[...]
    cs = _dot_mask_exact(incl.astype(BF16), ld)
    tot = jnp.sum(ld, axis=0, keepdims=True)
    g_end = jnp.exp(tot - cs)
    g_inv = jnp.exp(-cs)
    g_tot = jnp.exp(tot)
    be = be_ref[...]
    kd = kd_ref[...]
    ab = _bf(al_ref[...] * jnp.exp(cs - ld))
    qb = _bf(r_ref[...] * jnp.exp(cs))
    bt = _bf(be * g_inv)
    kt = _bf(kd * g_inv)
    b_end = _bf(be * g_end)
    k_end = _bf(kd * g_end)
    vb = _bf(v_ref[...])
    hs = range(nh)
    sls = [slice(h * hd, (h + 1) * hd) for h in hs]
    s_old = [s_scr[h] for h in hs]
    sb = [_bf(s) for s in s_old]
    aq = [jnp.concatenate([ab[:, sl], qb[:, sl]], axis=0) for sl in sls]
    xb = [_dot_nt(aq[h], bt[:, sls[h]]) for h in hs]
    xk = [_dot_nt(aq[h], kt[:, sls[h]]) for h in hs]
    t_inv = _tri_inverse([jnp.where(strict, -xh[:CH], 0.0) for xh in xb], masks, eye)
    a_ak = [_bf(jnp.where(strict, xh[:CH], 0.0)) for xh in xk]
    rhs = [_dot_nt(ab[:, sls[h]], sb[h]) + _dot(a_ak[h], vb[:, sls[h]]) for h in hs]
    u = [_bf(_dot(_bf(t_inv[h]), _bf(rhs[h]))) for h in hs]
    for h in hs:
        a_qb = _bf(jnp.where(incl, xb[h][CH:], 0.0))
        a_qk = _bf(jnp.where(incl, xk[h][CH:], 0.0))
        o_ref[:, sls[h]] = (_dot_nt(qb[:, sls[h]], sb[h]) + _dot(a_qb, u[h]) + _dot(a_qk, vb[:, sls[h]]))
    for h in hs:
        sl = sls[h]
        s_scr[h] = s_old[h] * g_tot[:, sl] + _dot_tn(u[h], b_end[:, sl]) + _dot_tn(vb[:, sl], k_end[:, sl])

    @pl.when(n == tbl_ref[2, t] - 1)
    def _():
        sT_ref[0, 0] = s_scr[...]


def _rw_scan(r, ld, alpha, be, kd, v, s0, tbl):
    m, d = r.shape
    nh, hd = s0.shape[2:4]
    shared = pl.BlockSpec((CH, d), _scan_row(0))
    perdir = pl.BlockSpec((CH, d), _scan_row(None))
    st = pl.BlockSpec((1, 1, nh, hd, hd), _scan_state)
    return pl.pallas_call(
        functools.partial(_rw_scan_kernel, nh=nh, hd=hd),
        grid_spec=pltpu.PrefetchScalarGridSpec(
            num_scalar_prefetch=1, grid=(2, tbl.shape[1]),
            in_specs=[shared, perdir, shared, perdir, perdir, shared, st],
            out_specs=[perdir, st],
            scratch_shapes=[pltpu.VMEM((nh, hd, hd), F32)]),
        out_shape=[jax.ShapeDtypeStruct((m, 2 * d), F32), jax.ShapeDtypeStruct(s0.shape, F32)],
        compiler_params=_cparams("parallel", "arbitrary"),
    )(tbl, r, ld, alpha, be, kd, v, s0)


def _rw_post_kernel(o_ref, bonus_ref, g_ref, gng_ref, gnb_ref, y_ref):
    d = y_ref.shape[1]
    o = o_ref[:, :d] + o_ref[:, d:]
    oc = o - _segsum(o, RW_HEAD) * (1.0 / RW_HEAD)
    var = _segsum(oc * oc, RW_HEAD) * (1.0 / RW_HEAD)
    y = oc * lax.rsqrt(var + RW_GN_EPS) * gng_ref[...] + gnb_ref[...]
    y_ref[...] = _bf((y + bonus_ref[...]) * g_ref[...])


def _rw_post(o, bonus, g, gn_g, gn_b):
    m, d = bonus.shape
    return pl.pallas_call(
        _rw_post_kernel,
        grid=(m // SEG,),
        in_specs=[_row_spec(2 * d), _row_spec(d), _row_spec(d), _const_spec((1, d)), _const_spec((1, d))],
        out_specs=_row_spec(d),
        out_shape=jax.ShapeDtypeStruct((m, d), BF16),
        compiler_params=_cparams("parallel"),
    )(o, bonus, g, gn_g.reshape(1, d), gn_b.reshape(1, d))


def _rwkv_mixer(h, state, p, dims):
    bp, bs, parts = dims
    m, d = h.shape
    r, v, g, ld, be, kd, alpha, bonus = _rw_proj(h, p, bp, parts)
    nh = d // RW_HEAD
    s0 = jnp.concatenate([jnp.zeros((2, bp, nh, RW_HEAD, RW_HEAD), F32), jnp.swapaxes(state, 0, 1)], axis=1)
    o, s_fin = _rw_scan(r, ld, alpha, be, kd, v, s0, _scan_table(bp, bs, parts))
    y = _rw_post(o, bonus, g, p["gn_g"], p["gn_b"])
    return y, jnp.swapaxes(s_fin[:, :bp], 0, 1)


def _rope(y, cos, sin):
    w = y.shape[1]
    up = pltpu.roll(y, w - AT_HEAD // 4, 1)
    dn = pltpu.roll(y, AT_HEAD // 4, 1)
    lane = lax.broadcasted_iota(jnp.int32, y.shape, 1)
    first = (lane % (AT_HEAD // 2)) < (AT_HEAD // 4)
    return y * cos + jnp.where(first, up, dn) * sin


def _at_pre_kernel(qkv_ref, cos_ref, sin_ref, qg_ref, kg_ref, q_ref, kr_ref, kn_ref, *, dq, dk):
    q = qkv_ref[:, :dq]
    k = qkv_ref[:, dq:dq + dk]
    qn = q * lax.rsqrt(_segsum(q * q, AT_HEAD) * (1.0 / AT_HEAD) + 1e-6) * qg_ref[...]
    kn = k * lax.rsqrt(_segsum(k * k, AT_HEAD) * (1.0 / AT_HEAD) + 1e-6) * kg_ref[...]
    kn_ref[...] = kn
    q_ref[...] = _bf(_rope(qn, cos_ref[...], sin_ref[...]))
    kr_ref[...] = _bf(_rope(kn, cos_ref[:, :dk], sin_ref[:, :dk]))


def _at_pre(qkv, cos_t, sin_t, qg, kg, bp, parts, dq, dk):
    m, w = qkv.shape
    tab = pl.BlockSpec((SEG, dq), lambda i: (jnp.where(i < bp, 0, 1 + (i - bp) % parts), 0))
    return pl.pallas_call(
        functools.partial(_at_pre_kernel, dq=dq, dk=dk),
        grid=(m // SEG,),
        in_specs=[_row_spec(w), tab, tab, _const_spec((1, dq)), _const_spec((1, dk))],
        out_specs=[_row_spec(dq), _row_spec(dk), _row_spec(dk)],
        out_shape=[jax.ShapeDtypeStruct((m, dq), BF16), jax.ShapeDtypeStruct((m, dk), BF16),
                   jax.ShapeDtypeStruct((m, dk), F32)],
        compiler_params=_cparams("parallel"),
    )(qkv, cos_t, sin_t, qg, kg)


def _attn_kernel(q_ref, k_ref, v_ref, o_ref, *, scale):
    nkv = k_ref.shape[2] // AT_HEAD
    for g in range(nkv):
        kg = k_ref[0, :, g * AT_HEAD:(g + 1) * AT_HEAD]
        vg = v_ref[0, :, g * AT_HEAD:(g + 1) * AT_HEAD]
        for j in range(AT_GROUP):
            sl = slice((g * AT_GROUP + j) * AT_HEAD, (g * AT_GROUP + j + 1) * AT_HEAD)
            s = _dot_nt(q_ref[:, sl], kg) * scale
            p = jnp.exp(s - jnp.max(s, axis=-1, keepdims=True))
            l = jnp.sum(p, axis=-1, keepdims=True)
            o_ref[:, sl] = (_dot(_bf(p), vg) / l).astype(o_ref.dtype)


def _attention(q, k, v, *, base, nq):
    dq = q.shape[1]
    b, tk, dk = k.shape
    kv = pl.BlockSpec((1, tk, dk), lambda bb, j: (bb, 0, 0))
    return pl.pallas_call(
        functools.partial(_attn_kernel, scale=AT_HEAD ** -0.5),
        grid=(b, nq),
        in_specs=[pl.BlockSpec((SEG, dq), lambda bb, j: (base + bb * nq + j, 0)), kv, kv],
        out_specs=pl.BlockSpec((SEG, dq), lambda bb, j: (bb * nq + j, 0)),
        out_shape=jax.ShapeDtypeStruct((b * nq * SEG, dq), BF16),
        compiler_params=_cparams("parallel", "parallel"),
    )(q, k, v)


def _rope_tables(t, heads):
    rows = t // GRID_W
    row = jnp.repeat(jnp.arange(rows), GRID_W)
    col = jnp.tile(jnp.arange(GRID_W), rows)
    n_freq = AT_HEAD // 4
    inv_freq = ROPE_THETA ** (-jnp.arange(n_freq, dtype=F32) / n_freq)
    ang = jnp.stack([row, col], axis=-1).astype(F32)[:, :, None] * inv_freq
    cos = jnp.repeat(jnp.cos(ang)[:, :, None, :], 2, axis=2).reshape(t, AT_HEAD)
    sin = jnp.sin(ang)
    sin = jnp.stack([-sin, sin], axis=2).reshape(t, AT_HEAD)
    ident = (jnp.ones((SEG, AT_HEAD), F32), jnp.zeros((SEG, AT_HEAD), F32))
    cos = jnp.tile(jnp.concatenate([ident[0], cos], axis=0), (1, heads))
    sin = jnp.tile(jnp.concatenate([ident[1], sin], axis=0), (1, heads))
    return cos, sin


def _attention_mixer(h, cache_k, cache_v, p, dims):
    bp, bs, parts = dims
    m, d = h.shape
    dk = cache_k.shape[-1] * cache_k.shape[-2]
    heads = d // AT_HEAD
    qkv = _mm(h, p["wqkv"])
    cos_t, sin_t = _rope_tables(parts * SEG, heads)
    q, kr, kn = _at_pre(qkv, cos_t, sin_t, jnp.tile(p["qn"], heads).reshape(1, d),
                        jnp.tile(p["kn"], dk // AT_HEAD).reshape(1, dk), bp, parts, d, dk)
    v = qkv[:, d + dk:]
    np_ = bp * SEG
    o_p = _attention(q, kr[:np_].reshape(bp, SEG, dk), _bf(v[:np_]).reshape(bp, SEG, dk), base=0, nq=1)
    k_s = jnp.concatenate([_bf(cache_k.reshape(bs, -1, dk)), kr[np_:].reshape(bs, parts * SEG, dk)], axis=1)
    v_s = jnp.concatenate([_bf(cache_v.reshape(bs, -1, dk)), _bf(v[np_:]).reshape(bs, parts * SEG, dk)], axis=1)
    o_s = _attention(q, k_s, v_s, base=bp, nq=parts)
    return jnp.concatenate([o_p, o_s], axis=0), kn[:np_], v[:np_]


def _dn_pre_kernel(x_ref, xp_ref, xn_ref, gt_ref, cw_ref, ga_ref, gb_ref, q_ref, k_ref, v_ref, gd_ref,
                   *, bp, parts, d, dk):
    is_start, is_end = _seg_flags(pl.program_id(0), bp, parts)
    x = x_ref[...]
    c = (cw_ref[0:1, :] * _shift_prev(x, xp_ref[...], is_start) + cw_ref[1:2, :] * x
         + cw_ref[2:3, :] * _shift_next(x, xn_ref[...], is_end))
    s = c * _sigmoid(c)
    q = s[:, :d]
    k = s[:, d:2 * d]
    q_ref[...] = q * lax.rsqrt(_segsum(q * q, dk) + 1e-6) * (dk ** -0.5)
    k_ref[...] = k * lax.rsqrt(_segsum(k * k, dk) + 1e-6)
    v_ref[...] = s[:, 2 * d:]
    gt = gt_ref[...]
    lane = lax.broadcasted_iota(jnp.int32, gt.shape, 1) % LANES
    gd_ref[...] = jnp.where(lane < DN_HEADS, -jnp.exp(ga_ref[...]) * _softplus(gt + gb_ref[...]), _sigmoid(gt))


def _dn_pre(proj, gates, conv_w, ga, gb, bp, parts, d, dk):
    m = proj.shape[0]
    w3 = 3 * d
    hp, hn = _halo_specs(w3, m)
    return pl.pallas_call(
        functools.partial(_dn_pre_kernel, bp=bp, parts=parts, d=d, dk=dk),
        grid=(m // SEG,),
        in_specs=[_row_spec(w3), hp, hn, _row_spec(2 * LANES), _const_spec((3, w3)),
                  _const_spec((1, 2 * LANES)), _const_spec((1, 2 * LANES))],
        out_specs=[_row_spec(d)] * 3 + [_row_spec(2 * LANES)],
        out_shape=[jax.ShapeDtypeStruct((m, d), F32)] * 3 + [jax.ShapeDtypeStruct((m, 2 * LANES), F32)],
        compiler_params=_cparams("parallel"),
    )(proj, proj, proj, gates, conv_w, ga, gb)


def _dn_scan_kernel(tbl_ref, q_ref, k_ref, v_ref, gd_ref, s0_ref, o_ref, sT_ref, s_scr, *, nh, hd):
    e = pl.program_id(0)
    t = pl.program_id(1)
    n = tbl_ref[1, t]

    @pl.when(n == 0)
    def _():
        s_scr[...] = s0_ref[0, 0]

    incl, strict, eye = _scan_masks(e)
    masks = _merge_masks(CH)
    gd = gd_ref[...]
    cs = _dot_mask_exact(incl.astype(BF16), gd)
    tot = jnp.sum(gd, axis=0, keepdims=True)
    e_cs = jnp.exp(cs)
    e_end = jnp.exp(tot - cs)
    e_tot = jnp.exp(tot)
    hs = range(nh)
    sls = [slice(h * hd, (h + 1) * hd) for h in hs]
    cs_t = jnp.concatenate([cs, cs], axis=0).T
    g_col = [jnp.broadcast_to(cs[:, h:h + 1], (CH, CH)) for h in hs]
    g_row = [jnp.broadcast_to(cs_t[h:h + 1, :CH], (CH, CH)) for h in hs]
    dec = [jnp.where(incl, jnp.exp(jnp.where(incl, g_col[h] - g_row[h], 0.0)), 0.0) for h in hs]
    beta = [gd[:, nh + h:nh + h + 1] for h in hs]
    kh = [k_ref[:, sl] for sl in sls]
    kb = [kh[h] * beta[h] for h in hs]
    khb = [_bf(x) for x in kh]
    qh = [q_ref[:, sl] for sl in sls]
    x = [_dot_nt(jnp.concatenate([_bf(kb[h]), _bf(qh[h])], axis=0), khb[h]) for h in hs]
    low = [jnp.where(strict, x[h][:CH] * dec[h], 0.0) for h in hs]
    t_inv = [_bf(t) for t in _tri_inverse(low, masks, eye)]
    vk = [jnp.concatenate([_bf(v_ref[:, sls[h]] * beta[h]), _bf(kb[h] * e_cs[:, h:h + 1])], axis=1) for h in hs]
    uw = [_dot(t_inv[h], vk[h]) for h in hs]
    s_old = [s_scr[h] for h in hs]
    sb = [_bf(s) for s in s_old]
    v_new = [_bf(uw[h][:, :hd] - _dot(_bf(uw[h][:, hd:]), sb[h])) for h in hs]
    for h in hs:
        qk = _bf(jnp.where(incl, x[h][CH:] * dec[h], 0.0))
        o_ref[:, sls[h]] = _dot(_bf(qh[h] * e_cs[:, h:h + 1]), sb[h]) + _dot(qk, v_new[h])
    for h in hs:
        s_scr[h] = (s_old[h] * e_tot[:, h:h + 1]
                    + _dot_tn(_bf(kh[h] * e_end[:, h:h + 1]), v_new[h]))

    @pl.when(n == tbl_ref[2, t] - 1)
    def _():
        sT_ref[0, 0] = s_scr[...]


def _dn_scan(q, k, v, gd, s0, tbl):
    m, d = q.shape
    nh, hd = s0.shape[2:4]
    shared = pl.BlockSpec((CH, d), _scan_row(0))
    gates = pl.BlockSpec((CH, LANES), _scan_row(None))
    perdir = pl.BlockSpec((CH, d), _scan_row(None))
    st = pl.BlockSpec((1, 1, nh, hd, hd), _scan_state)
    return pl.pallas_call(
        functools.partial(_dn_scan_kernel, nh=nh, hd=hd),
        grid_spec=pltpu.PrefetchScalarGridSpec(
            num_scalar_prefetch=1, grid=(2, tbl.shape[1]),
            in_specs=[shared, shared, shared, gates, st],
            out_specs=[perdir, st],
            scratch_shapes=[pltpu.VMEM((nh, hd, hd), F32)]),
        out_shape=[jax.ShapeDtypeStruct((m, 2 * d), F32), jax.ShapeDtypeStruct(s0.shape, F32)],
        compiler_params=_cparams("parallel", "arbitrary"),
    )(tbl, q, k, v, gd, s0)


def _dn_post_kernel(o_ref, z_ref, ng_ref, y_ref, *, dv):
    d = y_ref.shape[1]
    o = o_ref[:, :d] + o_ref[:, d:]
    z = z_ref[...]
    y = o * lax.rsqrt(_segsum(o * o, dv) * (1.0 / dv) + 1e-6) * ng_ref[...]
    y_ref[...] = _bf(y * (z * _sigmoid(z)))


def _dn_post(o, proj, ng, d, dv):
    m = o.shape[0]
    return pl.pallas_call(
        functools.partial(_dn_post_kernel, dv=dv),
        grid=(m // SEG,),
        in_specs=[_row_spec(2 * d), _row_spec(d, col=3), _const_spec((1, d))],
        out_specs=_row_spec(d),
        out_shape=jax.ShapeDtypeStruct((m, d), BF16),
        compiler_params=_cparams("parallel"),
    )(o, proj, ng)


def _deltanet_mixer(h, state, p, dims):
    bp, bs, parts = dims
    m, d = h.shape
    dk = d // DN_HEADS
    proj = _mm(h, p["win"])
    gates = _mm(h, p["wg"], tn=2 * LANES)
    q, k, v, gd = _dn_pre(proj, gates, p["conv"], p["ga"], p["gb"], bp, parts, d, dk)
    s0 = jnp.concatenate([jnp.zeros((2, bp, DN_HEADS, dk, dk), F32), jnp.swapaxes(state, 0, 1)], axis=1)
    o, s_fin = _dn_scan(q, k, v, gd, s0, _scan_table(bp, bs, parts))
    y = _dn_post(o, proj, jnp.tile(p["ng"], DN_HEADS).reshape(1, d), d, dk)
    return y, jnp.swapaxes(s_fin[:, :bp], 0, 1)


def _gate_columns(w_gate):
    dd = w_gate.shape[0]
    pad = jnp.zeros((dd, LANES - 2 * DN_HEADS), w_gate.dtype)
    cols = []
    for e in range(2):
        cols += [w_gate[:, e * DN_HEADS:(e + 1) * DN_HEADS],
                 w_gate[:, (2 + e) * DN_HEADS:(3 + e) * DN_HEADS], pad]
    return jnp.concatenate(cols, axis=1)


def _gate_lanes(x):
    pad = jnp.zeros((LANES - DN_HEADS,), x.dtype)
    return jnp.concatenate([x[0], pad, x[1], pad]).reshape(1, 2 * LANES)


def kernel(x_prompt, x_sample, state_rwkv, cache_k, cache_v, state_delta, c, c_ctx, w_mod, b_mod, ln_g, ln_b, w_fc1, w_fc2, rw_mu, rw_wrkv, rw_w0, rw_w1, rw_w2, rw_a0, rw_a1, rw_a2, rw_g1, rw_g2, rw_kk, rw_ka, rw_rk, rw_gn_g, rw_gn_b, rw_wo, at_wqkv, at_qn, at_kn, at_wo, dn_win, dn_conv, dn_alog, dn_dtb, dn_ng, dn_wo):
    bp, tp, d = x_prompt.shape
    bs, ts, _ = x_sample.shape
    assert tp == SEG and ts % SEG == 0 and bs + 1 <= SUBLANES
    parts = ts // SEG
    dims = (bp, bs, parts)
    depth = w_mod.shape[0]
    alpha = (2.0 * depth) ** 0.25

    x = jnp.concatenate([x_prompt.reshape(bp * tp, d), x_sample.reshape(bs * ts, d)], axis=0)
    cond = jnp.concatenate([c, c_ctx[None, :], jnp.zeros((SUBLANES - bs - 1, d), F32)], axis=0)
    mods = _modulation(cond, w_mod, b_mod)
    seg_row = jnp.array([bs] * bp + [b for b in range(bs) for _ in range(parts)], jnp.int32)
    modseg = [mods[l][seg_row].reshape(bp + bs * parts, N_MOD, d) for l in range(depth)]

    def mixer_in_dtype(l):
        return F32 if l % 3 == 0 else BF16

    new_rwkv, new_k, new_v, new_delta = [], [], [], []
    h = _premod(x, modseg[0], mixer_in_dtype(0))
    for l in range(depth):
        kind, j = l % 3, l // 3
        if kind == 0:
            z64 = jnp.zeros((RW_HEAD, d), F32)
            prm = dict(
                mu=rw_mu[j], wr=_bf(rw_wrkv[j, 0]), wk=_bf(rw_wrkv[j, 1]), wv=_bf(rw_wrkv[j, 2]),
                w1=_bf(jnp.concatenate([rw_w1[j, 0], rw_w1[j, 1]], axis=1)),
                w2=_bf(jnp.concatenate([jnp.concatenate([rw_w2[j, 0], z64], axis=1),
                                        jnp.concatenate([z64, rw_w2[j, 1]], axis=1)], axis=0)),
                a1=_bf(jnp.concatenate([rw_a1[j, 0], rw_a1[j, 1]], axis=1)),
                a2=_bf(jnp.concatenate([jnp.concatenate([rw_a2[j, 0], z64], axis=1),
                                        jnp.concatenate([z64, rw_a2[j, 1]], axis=1)], axis=0)),
                g1=_bf(rw_g1[j]), g2=_bf(rw_g2[j]), w0=rw_w0[j], a0=rw_a0[j],
                kk=rw_kk[j], ka=rw_ka[j], rk=rw_rk[j].reshape(-1), gn_g=rw_gn_g[j], gn_b=rw_gn_b[j],
                wo=_bf(rw_wo[j]))
            y, s_new = _rwkv_mixer(h, state_rwkv[:, j], prm, dims)
            new_rwkv.append(s_new)
        elif kind == 1:
            prm = dict(wqkv=_bf(at_wqkv[j]), qn=at_qn[j], kn=at_kn[j], wo=_bf(at_wo[j]))
            y, kp, vp = _attention_mixer(h, cache_k[:, j], cache_v[:, j], prm, dims)
            new_k.append(kp.reshape(bp, tp, -1, AT_HEAD))
            new_v.append(vp.reshape(bp, tp, -1, AT_HEAD))
        else:
            prm = dict(win=_bf(dn_win[j, :, :4 * d]), wg=_bf(_gate_columns(dn_win[j, :, 4 * d:])),
                       conv=dn_conv[j], ga=_gate_lanes(dn_alog[j]), gb=_gate_lanes(dn_dtb[j]),
                       ng=dn_ng[j], wo=_bf(dn_wo[j]))
            y, s_new = _deltanet_mixer(h, state_delta[:, j], prm, dims)
            new_delta.append(s_new)
        x, h = _sublayer_ln(_mm_ln_kernel, y, [prm["wo"]], x, modseg[l], modseg[l], ln_g[l, 0], ln_b[l, 0],
                            alpha=alpha, gate_idx=2, sh_idx=3, h_dtype=BF16)
        nxt = min(l + 1, depth - 1)
        x, h = _sublayer_ln(_mlp_ln_kernel, h, [_bf(w_fc1[l]), _bf(w_fc2[l])], x, modseg[l], modseg[nxt],
                            ln_g[l, 1], ln_b[l, 1], alpha=alpha, gate_idx=5, sh_idx=0,
                            h_dtype=mixer_in_dtype(l + 1) if l + 1 < depth else None, fc=d)
    y_prompt = x[:bp * tp].reshape(bp, tp, d)
    y_sample = x[bp * tp:].reshape(bs, ts, d)
    return (y_prompt, y_sample, jnp.stack(new_rwkv, axis=1), jnp.stack(new_k, axis=1),
            jnp.stack(new_v, axis=1), jnp.stack(new_delta, axis=1))
```

```python
import functools

import jax
import jax.numpy as jnp
import numpy as np
from jax import lax
from jax.experimental import pallas as pl
from jax.experimental.pallas import tpu as pltpu

F32 = jnp.float32
BF16 = jnp.bfloat16

SEG = 256
CH = 64
GRP = 4
LANES = 128
SUBLANES = 8
VMEM_LIMIT = 48 * 1024 * 1024

N_MOD = 6
LN_EPS = 1e-5
RW_HEAD = 64
RW_GN_EPS = 64e-5
AT_HEAD = 64
AT_GROUP = 4
GRID_W = 64
ROPE_THETA = 10000.0
DN_HEADS = 8


def _cparams(*sem):
    return pltpu.CompilerParams(dimension_semantics=sem, vmem_limit_bytes=VMEM_LIMIT)


def _dot(a, b):
    return jnp.dot(a, b, preferred_element_type=F32)


def _dot_nt(a, b):
    return lax.dot_general(a, b, (((1,), (1,)), ((), ())), preferred_element_type=F32)


def _dot_tn(a, b):
    return lax.dot_general(a, b, (((0,), (0,)), ((), ())), preferred_element_type=F32)


def _bf(x):
    return x.astype(BF16)


def _split3(x):
    hi = _bf(x)
    r1 = x - hi.astype(F32)
    mid = _bf(r1)
    lo = _bf(r1 - mid.astype(F32))
    return hi, mid, lo


def _dot_mask_exact(mask_bf, x):
    hi, mid, lo = _split3(x)
    return _dot(mask_bf, hi) + _dot(mask_bf, mid) + _dot(mask_bf, lo)


def _segsum(x, seg):
    w = x.shape[1]
    li = lax.broadcasted_iota(jnp.int32, (LANES, LANES), 0) // seg
    lj = lax.broadcasted_iota(jnp.int32, (LANES, LANES), 1) // seg
    bd = (li == lj).astype(BF16)
    outs = []
    for j in range(w // LANES):
        xs = x[:, LANES * j:LANES * (j + 1)]
        hi = _bf(xs)
        lo = _bf(xs - hi.astype(F32))
        outs.append(_dot(hi, bd) + _dot(lo, bd))
    return outs[0] if len(outs) == 1 else jnp.concatenate(outs, axis=1)


def _softplus(x):
    return jnp.maximum(x, 0.0) + jnp.log1p(jnp.exp(-jnp.abs(x)))


def _sigmoid(x):
    return 1.0 / (1.0 + jnp.exp(-x))


def _seg_flags(i, bp, parts):
    j = (i - bp) % parts
    is_ctx = i < bp
    return jnp.logical_or(is_ctx, j == 0), jnp.logical_or(is_ctx, j == parts - 1)


def _shift_prev(x, xp8, is_start):
    rolled = pltpu.roll(x, 1, 0)
    row0 = jnp.where(is_start, 0.0, xp8[SUBLANES - 1:SUBLANES, :])
    rid = lax.broadcasted_iota(jnp.int32, x.shape, 0)
    return jnp.where(rid == 0, row0, rolled)


def _shift_next(x, xn8, is_end):
    n = x.shape[0]
    rolled = pltpu.roll(x, n - 1, 0)
    rowl = jnp.where(is_end, 0.0, xn8[0:1, :])
    rid = lax.broadcasted_iota(jnp.int32, x.shape, 0)
    return jnp.where(rid == n - 1, rowl, rolled)


def _row_spec(w, col=0):
    return pl.BlockSpec((SEG, w), lambda i: (i, col))


def _halo_specs(w, m_rows):
    per = SEG // SUBLANES
    last = m_rows // SUBLANES - 1
    return (pl.BlockSpec((SUBLANES, w), lambda i: (jnp.maximum(i * per - 1, 0), 0)),
            pl.BlockSpec((SUBLANES, w), lambda i: (jnp.minimum((i + 1) * per, last), 0)))


def _const_spec(shape):
    nd = len(shape)
    return pl.BlockSpec(shape, lambda i: (0,) * nd)


def _segvec_spec(k, w):
    return pl.BlockSpec((1, k, w), lambda i: (i, 0, 0))


def _mm_kernel(x_ref, w_ref, o_ref, *, act):
    acc = _dot(_bf(x_ref[...]), w_ref[...])
    if act == "relu2":
        acc = jnp.square(jnp.maximum(acc, 0.0))
    o_ref[...] = acc.astype(o_ref.dtype)


def _mm(x, w, *, act=None, out_dtype=F32, tm=512, tn=512):
    m, k = x.shape
    n = w.shape[1]
    tn = min(tn, n)
    assert m % tm == 0 and n % tn == 0
    return pl.pallas_call(
        functools.partial(_mm_kernel, act=act),
        grid=(n // tn, m // tm),
        in_specs=[pl.BlockSpec((tm, k), lambda j, i: (i, 0)),
                  pl.BlockSpec((k, tn), lambda j, i: (0, j))],
        out_specs=pl.BlockSpec((tm, tn), lambda j, i: (i, j)),
        out_shape=jax.ShapeDtypeStruct((m, n), out_dtype),
        compiler_params=_cparams("parallel", "parallel"),
    )(x, w)


def _mod_kernel(c_ref, w_ref, b_ref, o_ref):
    c = c_ref[...]
    s = c * _sigmoid(c)
    o_ref[0] = _dot(_bf(s), _bf(w_ref[0])) + b_ref[0]


def _modulation(cond, w_mod, b_mod, tn=1536):
    depth, d, n = w_mod.shape
    return pl.pallas_call(
        _mod_kernel,
        grid=(depth, n // tn),
        in_specs=[pl.BlockSpec((SUBLANES, d), lambda l, j: (0, 0)),
                  pl.BlockSpec((1, d, tn), lambda l, j: (l, 0, j)),
                  pl.BlockSpec((1, 1, tn), lambda l, j: (l, 0, j))],
        out_specs=pl.BlockSpec((1, SUBLANES, tn), lambda l, j: (l, 0, j)),
        out_shape=jax.ShapeDtypeStruct((depth, SUBLANES, n), F32),
        compiler_params=_cparams("parallel", "parallel"),
    )(cond, w_mod, b_mod.reshape(depth, 1, n))


def _premod_kernel(x_ref, m_ref, h_ref):
    h = x_ref[...] * (1.0 + m_ref[0, 1:2, :]) + m_ref[0, 0:1, :]
    h_ref[...] = h.astype(h_ref.dtype)


def _premod(x, modseg, out_dtype):
    m, d = x.shape
    return pl.pallas_call(
        _premod_kernel,
        grid=(m // SEG,),
        in_specs=[_row_spec(d), _segvec_spec(N_MOD, d)],
        out_specs=_row_spec(d),
        out_shape=jax.ShapeDtypeStruct((m, d), out_dtype),
        compiler_params=_cparams("parallel"),
    )(x, modseg)


def _seg_rows(m_ref, idx, spt):
    d = m_ref.shape[2]
    rows = [jnp.broadcast_to(m_ref[s, idx:idx + 1, :], (SEG, d)) for s in range(spt)]
    return rows[0] if spt == 1 else jnp.concatenate(rows, axis=0)


def _ln_epilogue(x, delta, m_ref, mn_ref, g_ref, b_ref, xo_ref, ho_ref, *, alpha, gate_idx, sh_idx, spt):
    y = alpha * x + _seg_rows(m_ref, gate_idx, spt) * delta
    mu = jnp.mean(y, axis=-1, keepdims=True)
    yc = y - mu
    var = jnp.mean(yc * yc, axis=-1, keepdims=True)
    xn = yc * lax.rsqrt(var + LN_EPS) * g_ref[...] + b_ref[...]
    xo_ref[...] = xn
    if ho_ref is not None:
        h = xn * (1.0 + _seg_rows(mn_ref, sh_idx + 1, spt)) + _seg_rows(mn_ref, sh_idx, spt)
        ho_ref[...] = h.astype(ho_ref.dtype)


def _mm_ln_kernel(y_ref, w_ref, x_ref, m_ref, mn_ref, g_ref, b_ref, xo_ref, *rest, **kw):
    delta = _dot(y_ref[...], w_ref[...])
    _ln_epilogue(x_ref[...], delta, m_ref, mn_ref, g_ref, b_ref, xo_ref, rest[0] if rest else None, **kw)


def _mlp_ln_kernel(h_ref, w1_ref, w2_ref, x_ref, m_ref, mn_ref, g_ref, b_ref, xo_ref, *rest, fc, **kw):
    hb = h_ref[...]
    delta = jnp.zeros(x_ref.shape, F32)
    for j in range(w1_ref.shape[1] // fc):
        u = jnp.square(jnp.maximum(_dot(hb, w1_ref[:, j * fc:(j + 1) * fc]), 0.0))
        delta = delta + _dot(_bf(u), w2_ref[j * fc:(j + 1) * fc, :])
    _ln_epilogue(x_ref[...], delta, m_ref, mn_ref, g_ref, b_ref, xo_ref, rest[0] if rest else None, **kw)


def _resident(shape):
    nd = len(shape)
    return pl.BlockSpec(shape, lambda i: (0,) * nd, pipeline_mode=pl.Buffered(1))


def _sublayer_ln(kern, lead, weights, x, modseg, modseg_next, g, b, *, alpha, gate_idx, sh_idx, h_dtype,
                 **static):
    m, d = x.shape
    nseg = modseg.shape[0]
    spt = 2 if nseg % 2 == 0 else 1
    tm = spt * SEG
    row = lambda w: pl.BlockSpec((tm, w), lambda i: (i, 0))
    segv = pl.BlockSpec((spt, N_MOD, d), lambda i: (i, 0, 0))
    out_shape = [jax.ShapeDtypeStruct((m, d), F32)]
    out_specs = [row(d)]
    if h_dtype is not None:
        out_shape.append(jax.ShapeDtypeStruct((m, d), h_dtype))
        out_specs.append(row(d))
    res = pl.pallas_call(
        functools.partial(kern, alpha=alpha, gate_idx=gate_idx, sh_idx=sh_idx, spt=spt, **static),
        grid=(m // tm,),
        in_specs=[row(lead.shape[1])] + [_resident(w.shape) for w in weights]
        + [row(d), segv, segv, _const_spec((1, d)), _const_spec((1, d))],
        out_specs=out_specs,
        out_shape=out_shape,
        compiler_params=_cparams("parallel"),
    )(lead, *weights, x, modseg, modseg_next, g.reshape(1, d), b.reshape(1, d))
    return (res[0], res[1]) if h_dtype is not None else (res[0], None)


class _Packed:
    def __init__(self):
        gw = GRP * CH
        self.ti = lax.broadcasted_iota(jnp.int32, (CH, gw), 0)
        lane = lax.broadcasted_iota(jnp.int32, (CH, gw), 1)
        self.si = lane % CH
        self.grp = lane // CH
        self.eye = (self.ti == self.si).astype(F32)
        self.merge = []
        b = 1
        while b < CH:
            self.merge.append(jnp.logical_and(self.ti // (2 * b) == self.si // (2 * b),
                                              self.ti // b != self.si // b))
            b *= 2

    def bd(self, x):
        return jnp.concatenate([_bf(jnp.where(self.grp == j, x, 0.0)) for j in range(GRP)], axis=0)

    def tri_inverse(self, lows):
        ts = [self.eye - jnp.where(self.merge[0], low, 0.0) for low in lows]
        for mk in self.merge[1:]:
            xs = [_dot(_bf(jnp.where(mk, low, 0.0)), self.bd(t)) for low, t in zip(lows, ts)]
            ts = [t - _dot(_bf(t), self.bd(x)) for t, x in zip(ts, xs)]
        return ts


def _order(sign):
    return (lax.broadcasted_iota(jnp.int32, (CH, CH), 0) - lax.broadcasted_iota(jnp.int32, (CH, CH), 1)) * sign


def _scan_table(bp, bs, parts):
    ncp = SEG // CH
    ncs = parts * ncp
    steps = [(b * ncp, n, ncp, b) for b in range(bp) for n in range(ncp)]
    steps += [(bp * ncp + b * ncs, n, ncs, bp + b) for b in range(bs) for n in range(ncs)]
    return jnp.asarray(np.array(steps, np.int32).T)


def _dir_row(e, col):
    def index(t, tbl):
        n = tbl[1, t]
        return tbl[0, t] + n + e * (tbl[2, t] - 1 - 2 * n), col
    return index


def _rw_proj_kernel(h_ref, hp_ref, hn_ref, mu_ref, wr_ref, wk_ref, wv_ref, w1_ref, w2_ref, a1_ref, a2_ref,
                    g1_ref, g2_ref, w0_ref, a0_ref, kk_ref, ka_ref, rk_ref,
                    r_ref, v_ref, g_ref, ld_ref, be_ref, kd_ref, alpha_ref, bonus_ref, *, bp, parts):
    d = h_ref.shape[1]
    is_start, is_end = _seg_flags(pl.program_id(0), bp, parts)
    h = h_ref[...]
    xx = 0.5 * (_shift_prev(h, hp_ref[...], is_start) + _shift_next(h, hn_ref[...], is_end)) - h

    def mix(j):
        return _bf(h + xx * mu_ref[j:j + 1, :])

    r = _dot(mix(0), wr_ref[...])
    k = _dot(mix(2), wk_ref[...])
    v = _dot(mix(3), wv_ref[...])
    wl = _dot(_bf(jnp.tanh(_dot(mix(1), w1_ref[...]))), w2_ref[...])
    al = _dot(_bf(_dot(mix(4), a1_ref[...])), a2_ref[...])
    g_ref[...] = _dot(_bf(_sigmoid(_dot(mix(5), g1_ref[...]))), g2_ref[...])
    r_ref[...] = r
    v_ref[...] = v
    kkn = k * kk_ref[...]
    kk = kkn * lax.rsqrt(_segsum(kkn * kkn, RW_HEAD) + 1e-6)
    alpha_ref[...] = -kk
    bon = jnp.zeros_like(r)
    for e in range(2):
        cols = slice(e * d, (e + 1) * d)
        w_log = -_softplus(-(w0_ref[e:e + 1, :] + wl[:, cols])) - 0.5
        ld_ref[:, cols] = -jnp.exp(w_log)
        a = _sigmoid(a0_ref[e:e + 1, :] + al[:, cols])
        be_ref[:, cols] = kk * a
        kd = k * (1.0 + (a - 1.0) * ka_ref[...])
        kd_ref[:, cols] = kd
        bon = bon + r * kd * rk_ref[...]
    bonus_ref[...] = _segsum(bon, RW_HEAD) * v


def _rw_proj(h, p, bp, parts):
    m, d = h.shape
    hp, hn = _halo_specs(d, m)
    weights = [p[n] for n in ("wr", "wk", "wv", "w1", "w2", "a1", "a2", "g1", "g2")]
    vecs = [p["w0"], p["a0"], p["kk"].reshape(1, d), p["ka"].reshape(1, d), p["rk"].reshape(1, d)]
    return pl.pallas_call(
        functools.partial(_rw_proj_kernel, bp=bp, parts=parts),
        grid=(m // SEG,),
        in_specs=[_row_spec(d), hp, hn, _const_spec((6, d))] + [_resident(w.shape) for w in weights]
        + [_const_spec(x.shape) for x in vecs],
        out_specs=[_row_spec(d)] * 3 + [_row_spec(2 * d)] * 3 + [_row_spec(d)] * 2,
        out_shape=[jax.ShapeDtypeStruct((m, d), F32)] * 3 + [jax.ShapeDtypeStruct((m, 2 * d), F32)] * 3
        + [jax.ShapeDtypeStruct((m, d), F32)] * 2,
        compiler_params=_cparams("parallel"),
    )(h, h, h, p["mu"], *weights, *vecs)


def _rw_scan_kernel(tbl_ref, *refs, ng, hd):
    ins, s0_ref, outs, sT_ref, s_scr = refs[:12], refs[12], refs[13:15], refs[15], refs[16]
    t = pl.program_id(0)
    n = tbl_ref[1, t]

    @pl.when(n == 0)
    def _():
        s_scr[...] = s0_ref[0]

    assert hd == CH
    gw = GRP * hd
    pk = _Packed()
    bd = pk.bd
    own = (lax.broadcasted_iota(jnp.int32, (gw, gw), 0) // hd
           == lax.broadcasted_iota(jnp.int32, (gw, gw), 1) // hd)
    cols = lambda g: slice(g * gw, (g + 1) * gw)
    chains = []
    for e in range(2):
        r_ref, ld_ref, al_ref, be_ref, kd_ref, v_ref = ins[6 * e:6 * e + 6]
        sign = 1 - 2 * e
        ahead = (pk.ti - pk.si) * sign
        ld = ld_ref[...]
        cs = _dot_mask_exact((_order(sign) >= 0).astype(BF16), ld)
        tot = jnp.sum(ld, axis=0, keepdims=True)
        g_end = jnp.exp(tot - cs)
        g_inv = jnp.exp(-cs)
        be = be_ref[...]
        kd = kd_ref[...]
        full = dict(ab=al_ref[...] * jnp.exp(cs - ld), qb=r_ref[...] * jnp.exp(cs), bt=be * g_inv,
                    kt=kd * g_inv, b_end=be * g_end, k_end=kd * g_end, v=v_ref[...], g_tot=jnp.exp(tot))
        for g in range(ng):
            c = {k: x[:, cols(g)] for k, x in full.items()}
            c.update(e=e, g=g, incl=ahead >= 0, strict=ahead > 0, s_old=s_scr[e, g])
            chains.append(c)
    for c in chains:
        c["sb"] = _bf(c["s_old"])
        c["abb"] = _bf(c["ab"])
        c["qbb"] = _bf(c["qb"])
        c["vbd"] = bd(c["v"])
        aq = jnp.concatenate([c["abb"], c["qbb"]], axis=0)
        c["xb"] = _dot_nt(aq, bd(c["bt"]))
        c["xk"] = _dot_nt(aq, bd(c["kt"]))
    t_inv = pk.tri_inverse([jnp.where(c["strict"], -c["xb"][:CH], 0.0) for c in chains])
    for c, tc in zip(chains, t_inv):
        c["t"] = tc
    for c in chains:
        a_ak = _bf(jnp.where(c["strict"], c["xk"][:CH], 0.0))
        c["rhs"] = _dot_nt(c["abb"], c["sb"]) + _dot(a_ak, c["vbd"])
    for c in chains:
        c["u"] = _dot(_bf(c["t"]), bd(c["rhs"]))
    for c in chains:
        a_qb = _bf(jnp.where(c["incl"], c["xb"][CH:], 0.0))
        a_qk = _bf(jnp.where(c["incl"], c["xk"][CH:], 0.0))
        o = _dot_nt(c["qbb"], c["sb"]) + _dot(a_qb, bd(c["u"])) + _dot(a_qk, c["vbd"])
        outs[c["e"]][:, cols(c["g"])] = o
    for c in chains:
        uv = jnp.concatenate([_bf(c["u"]), _bf(c["v"])], axis=0)
        bk = jnp.concatenate([_bf(c["b_end"]), _bf(c["k_end"])], axis=0)
        s_scr[c["e"], c["g"]] = c["s_old"] * c["g_tot"] + jnp.where(own, _dot_tn(uv, bk), 0.0)

    @pl.when(n == tbl_ref[2, t] - 1)
    def _():
        sT_ref[0] = s_scr[...]


def _pack_states(s):
    n, _, nh, hd, _ = s.shape
    s = s.reshape(n, 2, nh // GRP, GRP, hd, hd)
    rows = []
    for j in range(GRP):
        z = lambda w: jnp.zeros(s.shape[:3] + (hd, w * hd), s.dtype)
        rows.append(jnp.concatenate([z(j), s[:, :, :, j], z(GRP - 1 - j)], axis=-1))
    return jnp.concatenate(rows, axis=-2)


def _unpack_states(sbd, hd):
    n, _, ng, _, _ = sbd.shape
    blocks = [sbd[:, :, :, j * hd:(j + 1) * hd, j * hd:(j + 1) * hd] for j in range(GRP)]
    return jnp.stack(blocks, axis=3).reshape(n, 2, ng * GRP, hd, hd)


def _rw_scan(r, ld, alpha, be, kd, v, s0, tbl):
    m, d = r.shape
    ng, gw = s0.shape[2:4]
    ins = []
    for e in range(2):
        ins += [pl.BlockSpec((CH, d), _dir_row(e, c)) for c in (0, e, 0, e, e, 0)]
    st = pl.BlockSpec((1, 2, ng, gw, gw), lambda t, tbl: (tbl[3, t], 0, 0, 0, 0))
    return pl.pallas_call(
        functools.partial(_rw_scan_kernel, ng=ng, hd=gw // GRP),
        grid_spec=pltpu.PrefetchScalarGridSpec(
            num_scalar_prefetch=1, grid=(tbl.shape[1],),
            in_specs=ins + [st],
            out_specs=[pl.BlockSpec((CH, d), _dir_row(0, 0)), pl.BlockSpec((CH, d), _dir_row(1, 0)), st],
            scratch_shapes=[pltpu.VMEM((2, ng, gw, gw), F32)]),
        out_shape=[jax.ShapeDtypeStruct((m, d), F32)] * 2 + [jax.ShapeDtypeStruct(s0.shape, F32)],
        compiler_params=_cparams("arbitrary"),
    )(tbl, *([r, ld, alpha, be, kd, v] * 2), s0)


def _rw_post_kernel(of_ref, ob_ref, bonus_ref, g_ref, gng_ref, gnb_ref, y_ref):
    o = of_ref[...] + ob_ref[...]
    oc = o - _segsum(o, RW_HEAD) * (1.0 / RW_HEAD)
    var = _segsum(oc * oc, RW_HEAD) * (1.0 / RW_HEAD)
    y = oc * lax.rsqrt(var + RW_GN_EPS) * gng_ref[...] + gnb_ref[...]
    y_ref[...] = _bf((y + bonus_ref[...]) * g_ref[...])


def _rw_post(o_f, o_b, bonus, g, gn_g, gn_b):
    m, d = bonus.shape
    return pl.pallas_call(
        _rw_post_kernel,
        grid=(m // SEG,),
        in_specs=[_row_spec(d)] * 4 + [_const_spec((1, d)), _const_spec((1, d))],
        out_specs=_row_spec(d),
        out_shape=jax.ShapeDtypeStruct((m, d), BF16),
        compiler_params=_cparams("parallel"),
    )(o_f, o_b, bonus, g, gn_g.reshape(1, d), gn_b.reshape(1, d))


def _rwkv_mixer(h, state, p, dims):
    bp, bs, parts = dims
    m, d = h.shape
    r, v, g, ld, be, kd, alpha, bonus = _rw_proj(h, p, bp, parts)
    s_lat = _pack_states(state)
    s0 = jnp.concatenate([jnp.zeros((bp,) + s_lat.shape[1:], F32), s_lat], axis=0)
    o_f, o_b, s_fin = _rw_scan(r, ld, alpha, be, kd, v, s0, _scan_table(bp, bs, parts))
    y = _rw_post(o_f, o_b, bonus, g, p["gn_g"], p["gn_b"])
    return y, _unpack_states(s_fin[:bp], RW_HEAD)


def _rope(y, cos, sin):
    w = y.shape[1]
    up = pltpu.roll(y, w - AT_HEAD // 4, 1)
    dn = pltpu.roll(y, AT_HEAD // 4, 1)
    lane = lax.broadcasted_iota(jnp.int32, y.shape, 1)
    first = (lane % (AT_HEAD // 2)) < (AT_HEAD // 4)
    return y * cos + jnp.where(first, up, dn) * sin


def _at_pre_kernel(qkv_ref, cos_ref, sin_ref, qg_ref, kg_ref, q_ref, kr_ref, kn_ref, *, dq, dk):
    q = qkv_ref[:, :dq]
    k = qkv_ref[:, dq:dq + dk]
    qn = q * lax.rsqrt(_segsum(q * q, AT_HEAD) * (1.0 / AT_HEAD) + 1e-6) * qg_ref[...]
    kn = k * lax.rsqrt(_segsum(k * k, AT_HEAD) * (1.0 / AT_HEAD) + 1e-6) * kg_ref[...]
    kn_ref[...] = kn
    q_ref[...] = _bf(_rope(qn, cos_ref[...], sin_ref[...]))
    kr_ref[...] = _bf(_rope(kn, cos_ref[:, :dk], sin_ref[:, :dk]))


def _at_pre(qkv, cos_t, sin_t, qg, kg, bp, parts, dq, dk):
    m, w = qkv.shape
    tab = pl.BlockSpec((SEG, dq), lambda i: (jnp.where(i < bp, 0, 1 + (i - bp) % parts), 0))
    return pl.pallas_call(
        functools.partial(_at_pre_kernel, dq=dq, dk=dk),
        grid=(m // SEG,),
        in_specs=[_row_spec(w), tab, tab, _const_spec((1, dq)), _const_spec((1, dk))],
        out_specs=[_row_spec(dq), _row_spec(dk), _row_spec(dk)],
        out_shape=[jax.ShapeDtypeStruct((m, dq), BF16), jax.ShapeDtypeStruct((m, dk), BF16),
                   jax.ShapeDtypeStruct((m, dk), F32)],
        compiler_params=_cparams("parallel"),
    )(qkv, cos_t, sin_t, qg, kg)


def _attn_kernel(q_ref, k_ref, v_ref, o_ref, *, scale):
    nkv = k_ref.shape[2] // AT_HEAD
    for g in range(nkv):
        kg = k_ref[0, :, g * AT_HEAD:(g + 1) * AT_HEAD]
        vg = v_ref[0, :, g * AT_HEAD:(g + 1) * AT_HEAD]
        for j in range(AT_GROUP):
            sl = slice((g * AT_GROUP + j) * AT_HEAD, (g * AT_GROUP + j + 1) * AT_HEAD)
            s = _dot_nt(q_ref[:, sl], kg) * scale
            p = jnp.exp(s - jnp.max(s, axis=-1, keepdims=True))
            l = jnp.sum(p, axis=-1, keepdims=True)
            o_ref[:, sl] = (_dot(_bf(p), vg) / l).astype(o_ref.dtype)


def _attention(q, k, v, *, base, nq):
    dq = q.shape[1]
    b, tk, dk = k.shape
    kv = pl.BlockSpec((1, tk, dk), lambda bb, j: (bb, 0, 0))
    return pl.pallas_call(
        functools.partial(_attn_kernel, scale=AT_HEAD ** -0.5),
        grid=(b, nq),
        in_specs=[pl.BlockSpec((SEG, dq), lambda bb, j: (base + bb * nq + j, 0)), kv, kv],
        out_specs=pl.BlockSpec((SEG, dq), lambda bb, j: (bb * nq + j, 0)),
        out_shape=jax.ShapeDtypeStruct((b * nq * SEG, dq), BF16),
        compiler_params=_cparams("parallel", "parallel"),
    )(q, k, v)


def _rope_tables(t, heads):
    rows = t // GRID_W
    row = jnp.repeat(jnp.arange(rows), GRID_W)
    col = jnp.tile(jnp.arange(GRID_W), rows)
    n_freq = AT_HEAD // 4
    inv_freq = ROPE_THETA ** (-jnp.arange(n_freq, dtype=F32) / n_freq)
    ang = jnp.stack([row, col], axis=-1).astype(F32)[:, :, None] * inv_freq
    cos = jnp.repeat(jnp.cos(ang)[:, :, None, :], 2, axis=2).reshape(t, AT_HEAD)
    sin = jnp.sin(ang)
    sin = jnp.stack([-sin, sin], axis=2).reshape(t, AT_HEAD)
    ident = (jnp.ones((SEG, AT_HEAD), F32), jnp.zeros((SEG, AT_HEAD), F32))
    cos = jnp.tile(jnp.concatenate([ident[0], cos], axis=0), (1, heads))
    sin = jnp.tile(jnp.concatenate([ident[1], sin], axis=0), (1, heads))
    return cos, sin


def _attention_mixer(h, cache_k, cache_v, p, dims):
    bp, bs, parts = dims
    m, d = h.shape
    dk = cache_k.shape[-1] * cache_k.shape[-2]
    heads = d // AT_HEAD
    qkv = _mm(h, p["wqkv"])
    cos_t, sin_t = _rope_tables(parts * SEG, heads)
    q, kr, kn = _at_pre(qkv, cos_t, sin_t, jnp.tile(p["qn"], heads).reshape(1, d),
                        jnp.tile(p["kn"], dk // AT_HEAD).reshape(1, dk), bp, parts, d, dk)
    v = qkv[:, d + dk:]
    np_ = bp * SEG
    o_p = _attention(q, kr[:np_].reshape(bp, SEG, dk), _bf(v[:np_]).reshape(bp, SEG, dk), base=0, nq=1)
    k_s = jnp.concatenate([_bf(cache_k.reshape(bs, -1, dk)), kr[np_:].reshape(bs, parts * SEG, dk)], axis=1)
    v_s = jnp.concatenate([_bf(cache_v.reshape(bs, -1, dk)), _bf(v[np_:]).reshape(bs, parts * SEG, dk)], axis=1)
    o_s = _attention(q, k_s, v_s, base=bp, nq=parts)
    return jnp.concatenate([o_p, o_s], axis=0), kn[:np_], v[:np_]


def _dn_pre_kernel(x_ref, xp_ref, xn_ref, gt_ref, cw_ref, ga_ref, gb_ref, q_ref, k_ref, v_ref, gd_ref,
                   *, bp, parts, d, dk):
    is_start, is_end = _seg_flags(pl.program_id(0), bp, parts)
    x = x_ref[...]
    c = (cw_ref[0:1, :] * _shift_prev(x, xp_ref[...], is_start) + cw_ref[1:2, :] * x
         + cw_ref[2:3, :] * _shift_next(x, xn_ref[...], is_end))
    s = c * _sigmoid(c)
    q = s[:, :d]
    k = s[:, d:2 * d]
    q_ref[...] = q * lax.rsqrt(_segsum(q * q, dk) + 1e-6) * (dk ** -0.5)
    k_ref[...] = k * lax.rsqrt(_segsum(k * k, dk) + 1e-6)
    v_ref[...] = s[:, 2 * d:]
    gt = gt_ref[...]
    lane = lax.broadcasted_iota(jnp.int32, gt.shape, 1) % LANES
    gd_ref[...] = jnp.where(lane < DN_HEADS, -jnp.exp(ga_ref[...]) * _softplus(gt + gb_ref[...]), _sigmoid(gt))


def _dn_pre(proj, gates, conv_w, ga, gb, bp, parts, d, dk):
    m = proj.shape[0]
    w3 = 3 * d
    hp, hn = _halo_specs(w3, m)
    return pl.pallas_call(
        functools.partial(_dn_pre_kernel, bp=bp, parts=parts, d=d, dk=dk),
        grid=(m // SEG,),
        in_specs=[_row_spec(w3), hp, hn, _row_spec(2 * LANES), _const_spec((3, w3)),
                  _const_spec((1, 2 * LANES)), _const_spec((1, 2 * LANES))],
        out_specs=[_row_spec(d)] * 3 + [_row_spec(2 * LANES)],
        out_shape=[jax.ShapeDtypeStruct((m, d), F32)] * 3 + [jax.ShapeDtypeStruct((m, 2 * LANES), F32)],
        compiler_params=_cparams("parallel"),
    )(proj, proj, proj, gates, conv_w, ga, gb)


def _dn_scan_kernel(tbl_ref, *refs, nh, hd):
    ins, s0_ref, outs, sT_ref, s_scr = refs[:8], refs[8], refs[9:11], refs[11], refs[12]
    t = pl.program_id(0)
    n = tbl_ref[1, t]

    @pl.when(n == 0)
    def _():
        s_scr[...] = s0_ref[0]

    pk = _Packed()
    chains = []
    for e in range(2):
        q_ref, k_ref, v_ref, gd_ref = ins[4 * e:4 * e + 4]
        order = _order(1 - 2 * e)
        incl = order >= 0
        gd = gd_ref[...]
        cs = _dot_mask_exact(incl.astype(BF16), gd)
        tot = jnp.sum(gd, axis=0, keepdims=True)
        e_cs = jnp.exp(cs)
        e_end = jnp.exp(tot - cs)
        e_tot = jnp.exp(tot)
        cs_t = jnp.concatenate([cs, cs], axis=0).T
        for h in range(nh):
            sl = slice(h * hd, (h + 1) * hd)
            diff = jnp.broadcast_to(cs[:, h:h + 1], (CH, CH)) - jnp.broadcast_to(cs_t[h:h + 1, :CH], (CH, CH))
            chains.append(dict(
                e=e, sl=sl, incl=incl, strict=order > 0, beta=gd[:, nh + h:nh + h + 1],
                dec=jnp.where(incl, jnp.exp(jnp.where(incl, diff, 0.0)), 0.0),
                kh=k_ref[:, sl], qh=q_ref[:, sl], vh=v_ref[:, sl], e_cs=e_cs[:, h:h + 1],
                e_end=e_end[:, h:h + 1], e_tot=e_tot[:, h:h + 1], s_old=s_scr[e, h]))
    for c in chains:
        c["kb"] = c["kh"] * c["beta"]
        c["x"] = _dot_nt(jnp.concatenate([_bf(c["kb"]), _bf(c["qh"])], axis=0), _bf(c["kh"]))
        c["low"] = jnp.where(c["strict"], c["x"][:CH] * c["dec"], 0.0)
    packs = [chains[i:i + GRP] for i in range(0, len(chains), GRP)]
    t_inv = pk.tri_inverse([jnp.concatenate([c["low"] for c in p], axis=1) for p in packs])
    for p, tp in zip(packs, t_inv):
        for j, c in enumerate(p):
            c["t"] = _bf(tp[:, j * CH:(j + 1) * CH])
    for c in chains:
        vk = jnp.concatenate([_bf(c["vh"] * c["beta"]), _bf(c["kb"] * c["e_cs"])], axis=1)
        c["uw"] = _dot(c["t"], vk)
        c["sb"] = _bf(c["s_old"])
    for c in chains:
        c["v_new"] = _bf(c["uw"][:, :hd] - _dot(_bf(c["uw"][:, hd:]), c["sb"]))
    for c in chains:
        qk = _bf(jnp.where(c["incl"], c["x"][CH:] * c["dec"], 0.0))
        outs[c["e"]][:, c["sl"]] = _dot(_bf(c["qh"] * c["e_cs"]), c["sb"]) + _dot(qk, c["v_new"])
    for c in chains:
        h = c["sl"].start // hd
        s_scr[c["e"], h] = c["s_old"] * c["e_tot"] + _dot_tn(_bf(c["kh"] * c["e_end"]), c["v_new"])

    @pl.when(n == tbl_ref[2, t] - 1)
    def _():
        sT_ref[0] = s_scr[...]


def _dn_scan(q, k, v, gd, s0, tbl):
    m, d = q.shape
    nh, hd = s0.shape[2:4]
    ins = []
    for e in range(2):
        ins += [pl.BlockSpec((CH, d), _dir_row(e, 0))] * 3 + [pl.BlockSpec((CH, LANES), _dir_row(e, e))]
    st = pl.BlockSpec((1, 2, nh, hd, hd), lambda t, tbl: (tbl[3, t], 0, 0, 0, 0))
    return pl.pallas_call(
        functools.partial(_dn_scan_kernel, nh=nh, hd=hd),
        grid_spec=pltpu.PrefetchScalarGridSpec(
            num_scalar_prefetch=1, grid=(tbl.shape[1],),
            in_specs=ins + [st],
            out_specs=[pl.BlockSpec((CH, d), _dir_row(0, 0)), pl.BlockSpec((CH, d), _dir_row(1, 0)), st],
            scratch_shapes=[pltpu.VMEM((2, nh, hd, hd), F32)]),
        out_shape=[jax.ShapeDtypeStruct((m, d), F32)] * 2 + [jax.ShapeDtypeStruct(s0.shape, F32)],
        compiler_params=_cparams("arbitrary"),
    )(tbl, *([q, k, v, gd] * 2), s0)


def _dn_post_kernel(of_ref, ob_ref, z_ref, ng_ref, y_ref, *, dv):
    o = of_ref[...] + ob_ref[...]
    z = z_ref[...]
    y = o * lax.rsqrt(_segsum(o * o, dv) * (1.0 / dv) + 1e-6) * ng_ref[...]
    y_ref[...] = _bf(y * (z * _sigmoid(z)))


def _dn_post(o_f, o_b, proj, ng, d, dv):
    m = o_f.shape[0]
    return pl.pallas_call(
        functools.partial(_dn_post_kernel, dv=dv),
        grid=(m // SEG,),
        in_specs=[_row_spec(d), _row_spec(d), _row_spec(d, col=3), _const_spec((1, d))],
        out_specs=_row_spec(d),
        out_shape=jax.ShapeDtypeStruct((m, d), BF16),
        compiler_params=_cparams("parallel"),
    )(o_f, o_b, proj, ng)


def _deltanet_mixer(h, state, p, dims):
    bp, bs, parts = dims
    m, d = h.shape
    dk = d // DN_HEADS
    proj = _mm(h, p["win"])
    gates = _mm(h, p["wg"], tn=2 * LANES)
    q, k, v, gd = _dn_pre(proj, gates, p["conv"], p["ga"], p["gb"], bp, parts, d, dk)
    s0 = jnp.concatenate([jnp.zeros((bp,) + state.shape[1:], F32), state], axis=0)
    o_f, o_b, s_fin = _dn_scan(q, k, v, gd, s0, _scan_table(bp, bs, parts))
    y = _dn_post(o_f, o_b, proj, jnp.tile(p["ng"], DN_HEADS).reshape(1, d), d, dk)
    return y, s_fin[:bp]


def _gate_columns(w_gate):
    dd = w_gate.shape[0]
    pad = jnp.zeros((dd, LANES - 2 * DN_HEADS), w_gate.dtype)
    cols = []
    for e in range(2):
        cols += [w_gate[:, e * DN_HEADS:(e + 1) * DN_HEADS],
                 w_gate[:, (2 + e) * DN_HEADS:(3 + e) * DN_HEADS], pad]
    return jnp.concatenate(cols, axis=1)


def _gate_lanes(x):
    pad = jnp.zeros((LANES - DN_HEADS,), x.dtype)
    return jnp.concatenate([x[0], pad, x[1], pad]).reshape(1, 2 * LANES)


def kernel(x_prompt, x_sample, state_rwkv, cache_k, cache_v, state_delta, c, c_ctx, w_mod, b_mod, ln_g, ln_b, w_fc1, w_fc2, rw_mu, rw_wrkv, rw_w0, rw_w1, rw_w2, rw_a0, rw_a1, rw_a2, rw_g1, rw_g2, rw_kk, rw_ka, rw_rk, rw_gn_g, rw_gn_b, rw_wo, at_wqkv, at_qn, at_kn, at_wo, dn_win, dn_conv, dn_alog, dn_dtb, dn_ng, dn_wo):
    bp, tp, d = x_prompt.shape
    bs, ts, _ = x_sample.shape
    assert tp == SEG and ts % SEG == 0 and bs + 1 <= SUBLANES
    parts = ts // SEG
    dims = (bp, bs, parts)
    depth = w_mod.shape[0]
    alpha = (2.0 * depth) ** 0.25

    x = jnp.concatenate([x_prompt.reshape(bp * tp, d), x_sample.reshape(bs * ts, d)], axis=0)
    cond = jnp.concatenate([c, c_ctx[None, :], jnp.zeros((SUBLANES - bs - 1, d), F32)], axis=0)
    mods = _modulation(cond, w_mod, b_mod)
    seg_row = jnp.array([bs] * bp + [b for b in range(bs) for _ in range(parts)], jnp.int32)
    modseg = [mods[l][seg_row].reshape(bp + bs * parts, N_MOD, d) for l in range(depth)]

    def mixer_in_dtype(l):
        return F32 if l % 3 == 0 else BF16

    new_rwkv, new_k, new_v, new_delta = [], [], [], []
    h = _premod(x, modseg[0], mixer_in_dtype(0))
    for l in range(depth):
        kind, j = l % 3, l // 3
        if kind == 0:
            z64 = jnp.zeros((RW_HEAD, d), F32)
            prm = dict(
                mu=rw_mu[j], wr=_bf(rw_wrkv[j, 0]), wk=_bf(rw_wrkv[j, 1]), wv=_bf(rw_wrkv[j, 2]),
                w1=_bf(jnp.concatenate([rw_w1[j, 0], rw_w1[j, 1]], axis=1)),
                w2=_bf(jnp.concatenate([jnp.concatenate([rw_w2[j, 0], z64], axis=1),
                                        jnp.concatenate([z64, rw_w2[j, 1]], axis=1)], axis=0)),
                a1=_bf(jnp.concatenate([rw_a1[j, 0], rw_a1[j, 1]], axis=1)),
                a2=_bf(jnp.concatenate([jnp.concatenate([rw_a2[j, 0], z64], axis=1),
                                        jnp.concatenate([z64, rw_a2[j, 1]], axis=1)], axis=0)),
                g1=_bf(rw_g1[j]), g2=_bf(rw_g2[j]), w0=rw_w0[j], a0=rw_a0[j],
                kk=rw_kk[j], ka=rw_ka[j], rk=rw_rk[j].reshape(-1), gn_g=rw_gn_g[j], gn_b=rw_gn_b[j],
                wo=_bf(rw_wo[j]))
            y, s_new = _rwkv_mixer(h, state_rwkv[:, j], prm, dims)
            new_rwkv.append(s_new)
        elif kind == 1:
            prm = dict(wqkv=_bf(at_wqkv[j]), qn=at_qn[j], kn=at_kn[j], wo=_bf(at_wo[j]))
            y, kp, vp = _attention_mixer(h, cache_k[:, j], cache_v[:, j], prm, dims)
            new_k.append(kp.reshape(bp, tp, -1, AT_HEAD))
            new_v.append(vp.reshape(bp, tp, -1, AT_HEAD))
        else:
            prm = dict(win=_bf(dn_win[j, :, :4 * d]), wg=_bf(_gate_columns(dn_win[j, :, 4 * d:])),
                       conv=dn_conv[j], ga=_gate_lanes(dn_alog[j]), gb=_gate_lanes(dn_dtb[j]),
                       ng=dn_ng[j], wo=_bf(dn_wo[j]))
            y, s_new = _deltanet_mixer(h, state_delta[:, j], prm, dims)
            new_delta.append(s_new)
        x, h = _sublayer_ln(_mm_ln_kernel, y, [prm["wo"]], x, modseg[l], modseg[l], ln_g[l, 0], ln_b[l, 0],
                            alpha=alpha, gate_idx=2, sh_idx=3, h_dtype=BF16)
        nxt = min(l + 1, depth - 1)
        x, h = _sublayer_ln(_mlp_ln_kernel, h, [_bf(w_fc1[l]), _bf(w_fc2[l])], x, modseg[l], modseg[nxt],
                            ln_g[l, 1], ln_b[l, 1], alpha=alpha, gate_idx=5, sh_idx=0,
                            h_dtype=mixer_in_dtype(l + 1) if l + 1 < depth else None, fc=d)
    y_prompt = x[:bp * tp].reshape(bp, tp, d)
    y_sample = x[bp * tp:].reshape(bs, ts, d)
    return (y_prompt, y_sample, jnp.stack(new_rwkv, axis=1), jnp.stack(new_k, axis=1),
            jnp.stack(new_v, axis=1), jnp.stack(new_delta, axis=1))
```

```python
import functools

import jax
import jax.numpy as jnp
import numpy as np
from jax import lax
from jax.experimental import pallas as pl
from jax.experimental.pallas import tpu as pltpu

F32 = jnp.float32
BF16 = jnp.bfloat16

SEG = 256
CH = 64
GRP = 4
LANES = 128
SUBLANES = 8
VMEM_LIMIT = 48 * 1024 * 1024

N_MOD = 6
LN_EPS = 1e-5
RW_HEAD = 64
RW_GN_EPS = 64e-5
AT_HEAD = 64
AT_GROUP = 4
GRID_W = 64
ROPE_THETA = 10000.0
DN_HEADS = 8


def _cparams(*sem):
    return pltpu.CompilerParams(dimension_semantics=sem, vmem_limit_bytes=VMEM_LIMIT)


def _dot(a, b):
    return jnp.dot(a, b, preferred_element_type=F32)


def _dot_nt(a, b):
    return lax.dot_general(a, b, (((1,), (1,)), ((), ())), preferred_element_type=F32)


def _dot_tn(a, b):
    return lax.dot_general(a, b, (((0,), (0,)), ((), ())), preferred_element_type=F32)


def _bf(x):
    return x.astype(BF16)


def _split3(x):
    hi = _bf(x)
    r1 = x - hi.astype(F32)
    mid = _bf(r1)
    lo = _bf(r1 - mid.astype(F32))
    return hi, mid, lo


def _dot_mask_exact(mask_bf, x):
    hi, mid, lo = _split3(x)
    return _dot(mask_bf, hi) + _dot(mask_bf, mid) + _dot(mask_bf, lo)


def _segsum(x, seg):
    w = x.shape[1]
    li = lax.broadcasted_iota(jnp.int32, (LANES, LANES), 0) // seg
    lj = lax.broadcasted_iota(jnp.int32, (LANES, LANES), 1) // seg
    bd = (li == lj).astype(BF16)
    outs = []
    for j in range(w // LANES):
        xs = x[:, LANES * j:LANES * (j + 1)]
        hi = _bf(xs)
        lo = _bf(xs - hi.astype(F32))
        outs.append(_dot(hi, bd) + _dot(lo, bd))
    return outs[0] if len(outs) == 1 else jnp.concatenate(outs, axis=1)


def _softplus(x):
    return jnp.maximum(x, 0.0) + jnp.log1p(jnp.exp(-jnp.abs(x)))


def _sigmoid(x):
    return 1.0 / (1.0 + jnp.exp(-x))


def _seg_flags(i, bp, parts):
    j = (i - bp) % parts
    is_ctx = i < bp
    return jnp.logical_or(is_ctx, j == 0), jnp.logical_or(is_ctx, j == parts - 1)


def _shift_prev(x, xp8, is_start):
    rolled = pltpu.roll(x, 1, 0)
    row0 = jnp.where(is_start, 0.0, xp8[SUBLANES - 1:SUBLANES, :])
    rid = lax.broadcasted_iota(jnp.int32, x.shape, 0)
    return jnp.where(rid == 0, row0, rolled)


def _shift_next(x, xn8, is_end):
    n = x.shape[0]
    rolled = pltpu.roll(x, n - 1, 0)
    rowl = jnp.where(is_end, 0.0, xn8[0:1, :])
    rid = lax.broadcasted_iota(jnp.int32, x.shape, 0)
    return jnp.where(rid == n - 1, rowl, rolled)


def _row_spec(w, col=0):
    return pl.BlockSpec((SEG, w), lambda i: (i, col))


def _halo_specs(w, m_rows, rows=SUBLANES):
    per = SEG // rows
    last = m_rows // rows - 1
    return (pl.BlockSpec((rows, w), lambda i: (jnp.maximum(i * per - 1, 0), 0)),
            pl.BlockSpec((rows, w), lambda i: (jnp.minimum((i + 1) * per, last), 0)))


def _const_spec(shape):
    nd = len(shape)
    return pl.BlockSpec(shape, lambda i: (0,) * nd)


def _segvec_spec(k, w):
    return pl.BlockSpec((1, k, w), lambda i: (i, 0, 0))


def _mod_kernel(c_ref, w_ref, b_ref, o_ref):
    c = c_ref[...]
    s = c * _sigmoid(c)
    o_ref[0] = _dot(_bf(s), _bf(w_ref[0])) + b_ref[0]


def _modulation(cond, w_mod, b_mod, tn=1536):
    depth, d, n = w_mod.shape
    return pl.pallas_call(
        _mod_kernel,
        grid=(depth, n // tn),
        in_specs=[pl.BlockSpec((SUBLANES, d), lambda l, j: (0, 0)),
                  pl.BlockSpec((1, d, tn), lambda l, j: (l, 0, j)),
                  pl.BlockSpec((1, 1, tn), lambda l, j: (l, 0, j))],
        out_specs=pl.BlockSpec((1, SUBLANES, tn), lambda l, j: (l, 0, j)),
        out_shape=jax.ShapeDtypeStruct((depth, SUBLANES, n), F32),
        compiler_params=_cparams("parallel", "parallel"),
    )(cond, w_mod, b_mod.reshape(depth, 1, n))


def _premod_kernel(x_ref, m_ref, h_ref):
    h = x_ref[...] * (1.0 + m_ref[0, 1:2, :]) + m_ref[0, 0:1, :]
    h_ref[...] = h.astype(h_ref.dtype)


def _premod(x, modseg, out_dtype):
    m, d = x.shape
    return pl.pallas_call(
        _premod_kernel,
        grid=(m // SEG,),
        in_specs=[_row_spec(d), _segvec_spec(N_MOD, d)],
        out_specs=_row_spec(d),
        out_shape=jax.ShapeDtypeStruct((m, d), out_dtype),
        compiler_params=_cparams("parallel"),
    )(x, modseg)


def _seg_rows(m_ref, idx, spt):
    d = m_ref.shape[2]
    rows = [jnp.broadcast_to(m_ref[s, idx:idx + 1, :], (SEG, d)) for s in range(spt)]
    return rows[0] if spt == 1 else jnp.concatenate(rows, axis=0)


def _ln_epilogue(x, delta, m_ref, mn_ref, g_ref, b_ref, xo_ref, ho_ref, *, alpha, gate_idx, sh_idx, spt):
    y = alpha * x + _seg_rows(m_ref, gate_idx, spt) * delta
    mu = jnp.mean(y, axis=-1, keepdims=True)
    yc = y - mu
    var = jnp.mean(yc * yc, axis=-1, keepdims=True)
    xn = yc * lax.rsqrt(var + LN_EPS) * g_ref[...] + b_ref[...]
    xo_ref[...] = xn
    if ho_ref is not None:
        h = xn * (1.0 + _seg_rows(mn_ref, sh_idx + 1, spt)) + _seg_rows(mn_ref, sh_idx, spt)
        ho_ref[...] = h.astype(ho_ref.dtype)


def _mm_ln_kernel(y_ref, w_ref, x_ref, m_ref, mn_ref, g_ref, b_ref, xo_ref, *rest, **kw):
    delta = _dot(y_ref[...], w_ref[...])
    _ln_epilogue(x_ref[...], delta, m_ref, mn_ref, g_ref, b_ref, xo_ref, rest[0] if rest else None, **kw)


def _mlp_ln_kernel(h_ref, w1_ref, w2_ref, x_ref, m_ref, mn_ref, g_ref, b_ref, xo_ref, *rest, fc, **kw):
    hb = h_ref[...]
    delta = jnp.zeros(x_ref.shape, F32)
    for j in range(w1_ref.shape[1] // fc):
        u = jnp.square(jnp.maximum(_dot(hb, w1_ref[:, j * fc:(j + 1) * fc]), 0.0))
        delta = delta + _dot(_bf(u), w2_ref[j * fc:(j + 1) * fc, :])
    _ln_epilogue(x_ref[...], delta, m_ref, mn_ref, g_ref, b_ref, xo_ref, rest[0] if rest else None, **kw)


def _resident(shape):
    nd = len(shape)
    return pl.BlockSpec(shape, lambda i: (0,) * nd, pipeline_mode=pl.Buffered(1))


def _sublayer_ln(kern, lead, weights, x, modseg, modseg_next, g, b, *, alpha, gate_idx, sh_idx, h_dtype,
                 **static):
    m, d = x.shape
    nseg = modseg.shape[0]
    spt = 2 if nseg % 2 == 0 else 1
    tm = spt * SEG
    row = lambda w: pl.BlockSpec((tm, w), lambda i: (i, 0))
    segv = pl.BlockSpec((spt, N_MOD, d), lambda i: (i, 0, 0))
    out_shape = [jax.ShapeDtypeStruct((m, d), F32)]
    out_specs = [row(d)]
    if h_dtype is not None:
        out_shape.append(jax.ShapeDtypeStruct((m, d), h_dtype))
        out_specs.append(row(d))
    res = pl.pallas_call(
        functools.partial(kern, alpha=alpha, gate_idx=gate_idx, sh_idx=sh_idx, spt=spt, **static),
        grid=(m // tm,),
        in_specs=[row(lead.shape[1])] + [_resident(w.shape) for w in weights]
        + [row(d), segv, segv, _const_spec((1, d)), _const_spec((1, d))],
        out_specs=out_specs,
        out_shape=out_shape,
        compiler_params=_cparams("parallel"),
    )(lead, *weights, x, modseg, modseg_next, g.reshape(1, d), b.reshape(1, d))
    return (res[0], res[1]) if h_dtype is not None else (res[0], None)


class _Packed:
    def __init__(self):
        gw = GRP * CH
        self.ti = lax.broadcasted_iota(jnp.int32, (CH, gw), 0)
        lane = lax.broadcasted_iota(jnp.int32, (CH, gw), 1)
        self.si = lane % CH
        self.grp = lane // CH
        self.eye = (self.ti == self.si).astype(F32)
        self.merge = []
        b = 1
        while b < CH:
            self.merge.append(jnp.logical_and(self.ti // (2 * b) == self.si // (2 * b),
                                              self.ti // b != self.si // b))
            b *= 2

    def bd(self, x):
        return jnp.concatenate([_bf(jnp.where(self.grp == j, x, 0.0)) for j in range(GRP)], axis=0)

    def tri_inverse(self, lows):
        ts = [self.eye - jnp.where(self.merge[0], low, 0.0) for low in lows]
        for mk in self.merge[1:]:
            xs = [_dot(_bf(jnp.where(mk, low, 0.0)), self.bd(t)) for low, t in zip(lows, ts)]
            ts = [t - _dot(_bf(t), self.bd(x)) for t, x in zip(ts, xs)]
        return ts


def _order(sign):
    return (lax.broadcasted_iota(jnp.int32, (CH, CH), 0) - lax.broadcasted_iota(jnp.int32, (CH, CH), 1)) * sign


def _scan_table(bp, bs, parts):
    ncp = SEG // CH
    ncs = parts * ncp
    steps = [(b * ncp, n, ncp, b) for b in range(bp) for n in range(ncp)]
    steps += [(bp * ncp + b * ncs, n, ncs, bp + b) for b in range(bs) for n in range(ncs)]
    return jnp.asarray(np.array(steps, np.int32).T)


def _dir_row(e, col):
    def index(t, tbl):
        n = tbl[1, t]
        return tbl[0, t] + n + e * (tbl[2, t] - 1 - 2 * n), col
    return index


def _rw_proj_kernel(h_ref, hp_ref, hn_ref, mu_ref, wr_ref, wk_ref, wv_ref, w1_ref, w2_ref, a1_ref, a2_ref,
                    g1_ref, g2_ref, w0_ref, a0_ref, kk_ref, ka_ref, rk_ref,
                    r_ref, v_ref, g_ref, ld_ref, be_ref, kd_ref, alpha_ref, bonus_ref, *, bp, parts):
    d = h_ref.shape[1]
    is_start, is_end = _seg_flags(pl.program_id(0), bp, parts)
    h = h_ref[...]
    xx = 0.5 * (_shift_prev(h, hp_ref[...], is_start) + _shift_next(h, hn_ref[...], is_end)) - h

    def mix(j):
        return _bf(h + xx * mu_ref[j:j + 1, :])

    r = _dot(mix(0), wr_ref[...])
    k = _dot(mix(2), wk_ref[...])
    v = _dot(mix(3), wv_ref[...])
    wl = _dot(_bf(jnp.tanh(_dot(mix(1), w1_ref[...]))), w2_ref[...])
    al = _dot(_bf(_dot(mix(4), a1_ref[...])), a2_ref[...])
    g_ref[...] = _dot(_bf(_sigmoid(_dot(mix(5), g1_ref[...]))), g2_ref[...])
    r_ref[...] = r
    v_ref[...] = v
    kkn = k * kk_ref[...]
    kk = kkn * lax.rsqrt(_segsum(kkn * kkn, RW_HEAD) + 1e-6)
    alpha_ref[...] = -kk
    bon = jnp.zeros_like(r)
    for e in range(2):
        cols = slice(e * d, (e + 1) * d)
        ld_ref[:, cols] = -float(np.exp(-0.5)) * _sigmoid(w0_ref[e:e + 1, :] + wl[:, cols])
        a = _sigmoid(a0_ref[e:e + 1, :] + al[:, cols])
        be_ref[:, cols] = kk * a
        kd = k * (1.0 + (a - 1.0) * ka_ref[...])
        kd_ref[:, cols] = kd
        bon = bon + r * kd * rk_ref[...]
    bonus_ref[...] = _segsum(bon, RW_HEAD) * v


def _rw_proj(h, p, bp, parts):
    m, d = h.shape
    hp, hn = _halo_specs(d, m)
    weights = [p[n] for n in ("wr", "wk", "wv", "w1", "w2", "a1", "a2", "g1", "g2")]
    vecs = [p["w0"], p["a0"], p["kk"].reshape(1, d), p["ka"].reshape(1, d), p["rk"].reshape(1, d)]
    return pl.pallas_call(
        functools.partial(_rw_proj_kernel, bp=bp, parts=parts),
        grid=(m // SEG,),
        in_specs=[_row_spec(d), hp, hn, _const_spec((6, d))] + [_resident(w.shape) for w in weights]
        + [_const_spec(x.shape) for x in vecs],
        out_specs=[_row_spec(d)] * 3 + [_row_spec(2 * d)] * 3 + [_row_spec(d)] * 2,
        out_shape=[jax.ShapeDtypeStruct((m, d), F32)] * 3 + [jax.ShapeDtypeStruct((m, 2 * d), F32)] * 3
        + [jax.ShapeDtypeStruct((m, d), F32)] * 2,
        compiler_params=_cparams("parallel"),
    )(h, h, h, p["mu"], *weights, *vecs)


def _rw_scan_kernel(tbl_ref, *refs, ng, hd):
    ins, s0_ref, outs, sT_ref, s_scr = refs[:12], refs[12], refs[13:15], refs[15], refs[16]
    t = pl.program_id(0)
    n = tbl_ref[1, t]

    @pl.when(n == 0)
    def _():
        s_scr[...] = s0_ref[0]

    assert hd == CH
    gw = GRP * hd
    pk = _Packed()
    bd = pk.bd
    own = (lax.broadcasted_iota(jnp.int32, (gw, gw), 0) // hd
           == lax.broadcasted_iota(jnp.int32, (gw, gw), 1) // hd)
    cols = lambda g: slice(g * gw, (g + 1) * gw)
    chains = []
    for e in range(2):
        r_ref, ld_ref, al_ref, be_ref, kd_ref, v_ref = ins[6 * e:6 * e + 6]
        sign = 1 - 2 * e
        ahead = (pk.ti - pk.si) * sign
        ld = ld_ref[...]
        cs = _dot_mask_exact((_order(sign) >= 0).astype(BF16), ld)
        tot = jnp.sum(ld, axis=0, keepdims=True)
        g_end = jnp.exp(tot - cs)
        g_inv = jnp.exp(-cs)
        be = be_ref[...]
        kd = kd_ref[...]
        full = dict(ab=al_ref[...] * jnp.exp(cs - ld), qb=r_ref[...] * jnp.exp(cs), bt=be * g_inv,
                    kt=kd * g_inv, b_end=be * g_end, k_end=kd * g_end, v=v_ref[...], g_tot=jnp.exp(tot))
        for g in range(ng):
            c = {k: x[:, cols(g)] for k, x in full.items()}
            c.update(e=e, g=g, incl=ahead >= 0, strict=ahead > 0, s_old=s_scr[e, g])
            chains.append(c)
    for c in chains:
        c["sb"] = _bf(c["s_old"])
        c["abb"] = _bf(c["ab"])
        c["qbb"] = _bf(c["qb"])
        c["vbd"] = bd(c["v"])
        aq = jnp.concatenate([c["abb"], c["qbb"]], axis=0)
        c["xb"] = _dot_nt(aq, bd(c["bt"]))
        c["xk"] = _dot_nt(aq, bd(c["kt"]))
        c["sq"] = _dot_nt(aq, c["sb"])
    for c in chains:
        a_k = jnp.concatenate([jnp.where(c["strict"], c["xk"][:CH], 0.0),
                               jnp.where(c["incl"], c["xk"][CH:], 0.0)], axis=0)
        c["av"] = _dot(_bf(a_k), c["vbd"])
    t_inv = pk.tri_inverse([jnp.where(c["strict"], -c["xb"][:CH], 0.0) for c in chains])
    for c, tc in zip(chains, t_inv):
        c["u"] = _dot(_bf(tc), bd(c["sq"][:CH] + c["av"][:CH]))
    for c in chains:
        a_qb = _bf(jnp.where(c["incl"], c["xb"][CH:], 0.0))
        outs[c["e"]][:, cols(c["g"])] = c["sq"][CH:] + c["av"][CH:] + _dot(a_qb, bd(c["u"]))
    for c in chains:
        uv = jnp.concatenate([_bf(c["u"]), _bf(c["v"])], axis=0)
        bk = jnp.concatenate([_bf(c["b_end"]), _bf(c["k_end"])], axis=0)
        s_scr[c["e"], c["g"]] = c["s_old"] * c["g_tot"] + jnp.where(own, _dot_tn(uv, bk), 0.0)

    @pl.when(n == tbl_ref[2, t] - 1)
    def _():
        sT_ref[0] = s_scr[...]


def _pack_states(s):
    n, _, nh, hd, _ = s.shape
    s = s.reshape(n, 2, nh // GRP, GRP, hd, hd)
    rows = []
    for j in range(GRP):
        z = lambda w: jnp.zeros(s.shape[:3] + (hd, w * hd), s.dtype)
        rows.append(jnp.concatenate([z(j), s[:, :, :, j], z(GRP - 1 - j)], axis=-1))
    return jnp.concatenate(rows, axis=-2)


def _unpack_states(sbd, hd):
    n, _, ng, _, _ = sbd.shape
    blocks = [sbd[:, :, :, j * hd:(j + 1) * hd, j * hd:(j + 1) * hd] for j in range(GRP)]
    return jnp.stack(blocks, axis=3).reshape(n, 2, ng * GRP, hd, hd)


def _rw_scan(r, ld, alpha, be, kd, v, s0, tbl):
    m, d = r.shape
    ng, gw = s0.shape[2:4]
    ins = []
    for e in range(2):
        ins += [pl.BlockSpec((CH, d), _dir_row(e, c)) for c in (0, e, 0, e, e, 0)]
    st = pl.BlockSpec((1, 2, ng, gw, gw), lambda t, tbl: (tbl[3, t], 0, 0, 0, 0))
    return pl.pallas_call(
        functools.partial(_rw_scan_kernel, ng=ng, hd=gw // GRP),
        grid_spec=pltpu.PrefetchScalarGridSpec(
            num_scalar_prefetch=1, grid=(tbl.shape[1],),
            in_specs=ins + [st],
            out_specs=[pl.BlockSpec((CH, d), _dir_row(0, 0)), pl.BlockSpec((CH, d), _dir_row(1, 0)), st],
            scratch_shapes=[pltpu.VMEM((2, ng, gw, gw), F32)]),
        out_shape=[jax.ShapeDtypeStruct((m, d), F32)] * 2 + [jax.ShapeDtypeStruct(s0.shape, F32)],
        compiler_params=_cparams("arbitrary"),
    )(tbl, *([r, ld, alpha, be, kd, v] * 2), s0)


def _rw_post_kernel(of_ref, ob_ref, bonus_ref, g_ref, gng_ref, gnb_ref, y_ref):
    o = of_ref[...] + ob_ref[...]
    oc = o - _segsum(o, RW_HEAD) * (1.0 / RW_HEAD)
    var = _segsum(oc * oc, RW_HEAD) * (1.0 / RW_HEAD)
    y = oc * lax.rsqrt(var + RW_GN_EPS) * gng_ref[...] + gnb_ref[...]
    y_ref[...] = _bf((y + bonus_ref[...]) * g_ref[...])


def _rw_post(o_f, o_b, bonus, g, gn_g, gn_b):
    m, d = bonus.shape
    return pl.pallas_call(
        _rw_post_kernel,
        grid=(m // SEG,),
        in_specs=[_row_spec(d)] * 4 + [_const_spec((1, d)), _const_spec((1, d))],
        out_specs=_row_spec(d),
        out_shape=jax.ShapeDtypeStruct((m, d), BF16),
        compiler_params=_cparams("parallel"),
    )(o_f, o_b, bonus, g, gn_g.reshape(1, d), gn_b.reshape(1, d))


def _rwkv_mixer(h, state, p, dims):
    bp, bs, parts = dims
    m, d = h.shape
    r, v, g, ld, be, kd, alpha, bonus = _rw_proj(h, p, bp, parts)
    s_lat = _pack_states(state)
    s0 = jnp.concatenate([jnp.zeros((bp,) + s_lat.shape[1:], F32), s_lat], axis=0)
    o_f, o_b, s_fin = _rw_scan(r, ld, alpha, be, kd, v, s0, _scan_table(bp, bs, parts))
    y = _rw_post(o_f, o_b, bonus, g, p["gn_g"], p["gn_b"])
    return y, _unpack_states(s_fin[:bp], RW_HEAD)


def _rope(y, cos, sin):
    w = y.shape[1]
    up = pltpu.roll(y, w - AT_HEAD // 4, 1)
    dn = pltpu.roll(y, AT_HEAD // 4, 1)
    lane = lax.broadcasted_iota(jnp.int32, y.shape, 1)
    first = (lane % (AT_HEAD // 2)) < (AT_HEAD // 4)
    return y * cos + jnp.where(first, up, dn) * sin


def _at_proj_kernel(h_ref, w_ref, cos_ref, sin_ref, qg_ref, kg_ref, q_ref, kr_ref, kn_ref, v_ref, vb_ref,
                    *, dq, dk):
    q = _dot(h_ref[...], w_ref[:, :dq])
    k = _dot(h_ref[...], w_ref[:, dq:dq + dk])
    v = _dot(h_ref[...], w_ref[:, dq + dk:])
    qn = q * lax.rsqrt(_segsum(q * q, AT_HEAD) * (1.0 / AT_HEAD) + 1e-6) * qg_ref[...]
    kn = k * lax.rsqrt(_segsum(k * k, AT_HEAD) * (1.0 / AT_HEAD) + 1e-6) * kg_ref[...]
    kn_ref[...] = kn
    v_ref[...] = v
    vb_ref[...] = _bf(v)
    q_ref[...] = _bf(_rope(qn, cos_ref[...], sin_ref[...]))
    kr_ref[...] = _bf(_rope(kn, cos_ref[:, :dk], sin_ref[:, :dk]))


def _at_proj(h, wqkv, cos_t, sin_t, qg, kg, bp, parts, dq, dk):
    m, d = h.shape
    tab = pl.BlockSpec((SEG, dq), lambda i: (jnp.where(i < bp, 0, 1 + (i - bp) % parts), 0))
    return pl.pallas_call(
        functools.partial(_at_proj_kernel, dq=dq, dk=dk),
        grid=(m // SEG,),
        in_specs=[_row_spec(d), _resident(wqkv.shape), tab, tab, _const_spec((1, dq)), _const_spec((1, dk))],
        out_specs=[_row_spec(dq)] + [_row_spec(dk)] * 4,
        out_shape=[jax.ShapeDtypeStruct((m, dq), BF16), jax.ShapeDtypeStruct((m, dk), BF16),
                   jax.ShapeDtypeStruct((m, dk), F32), jax.ShapeDtypeStruct((m, dk), F32),
                   jax.ShapeDtypeStruct((m, dk), BF16)],
        compiler_params=_cparams("parallel"),
    )(h, wqkv, cos_t, sin_t, qg, kg)


def _attn_kernel(q_ref, k_ref, v_ref, o_ref, *, scale):
    nkv = k_ref.shape[2] // AT_HEAD
    for g in range(nkv):
        kg = k_ref[0, :, g * AT_HEAD:(g + 1) * AT_HEAD]
        vg = v_ref[0, :, g * AT_HEAD:(g + 1) * AT_HEAD]
        for j in range(AT_GROUP):
            sl = slice((g * AT_GROUP + j) * AT_HEAD, (g * AT_GROUP + j + 1) * AT_HEAD)
            s = _dot_nt(q_ref[:, sl], kg) * scale
            p = jnp.exp(s - jnp.max(s, axis=-1, keepdims=True))
            l = jnp.sum(p, axis=-1, keepdims=True)
            o_ref[:, sl] = (_dot(_bf(p), vg) / l).astype(o_ref.dtype)


def _attention(q, k, v, *, base, nq):
    dq = q.shape[1]
    b, tk, dk = k.shape
    kv = pl.BlockSpec((1, tk, dk), lambda bb, j: (bb, 0, 0))
    return pl.pallas_call(
        functools.partial(_attn_kernel, scale=AT_HEAD ** -0.5),
        grid=(b, nq),
        in_specs=[pl.BlockSpec((SEG, dq), lambda bb, j: (base + bb * nq + j, 0)), kv, kv],
        out_specs=pl.BlockSpec((SEG, dq), lambda bb, j: (bb * nq + j, 0)),
        out_shape=jax.ShapeDtypeStruct((b * nq * SEG, dq), BF16),
        compiler_params=_cparams("parallel", "parallel"),
    )(q, k, v)


def _rope_tables(t, heads):
    rows = t // GRID_W
    row = jnp.repeat(jnp.arange(rows), GRID_W)
    col = jnp.tile(jnp.arange(GRID_W), rows)
    n_freq = AT_HEAD // 4
    inv_freq = ROPE_THETA ** (-jnp.arange(n_freq, dtype=F32) / n_freq)
    ang = jnp.stack([row, col], axis=-1).astype(F32)[:, :, None] * inv_freq
    cos = jnp.repeat(jnp.cos(ang)[:, :, None, :], 2, axis=2).reshape(t, AT_HEAD)
    sin = jnp.sin(ang)
    sin = jnp.stack([-sin, sin], axis=2).reshape(t, AT_HEAD)
    ident = (jnp.ones((SEG, AT_HEAD), F32), jnp.zeros((SEG, AT_HEAD), F32))
    cos = jnp.tile(jnp.concatenate([ident[0], cos], axis=0), (1, heads))
    sin = jnp.tile(jnp.concatenate([ident[1], sin], axis=0), (1, heads))
    return cos, sin


def _attention_mixer(h, cache_k, cache_v, p, dims):
    bp, bs, parts = dims
    m, d = h.shape
    dk = cache_k.shape[-1] * cache_k.shape[-2]
    heads = d // AT_HEAD
    cos_t, sin_t = _rope_tables(parts * SEG, heads)
    q, kr, kn, v, vb = _at_proj(h, p["wqkv"], cos_t, sin_t, jnp.tile(p["qn"], heads).reshape(1, d),
                                jnp.tile(p["kn"], dk // AT_HEAD).reshape(1, dk), bp, parts, d, dk)
    np_ = bp * SEG
    o_p = _attention(q, kr[:np_].reshape(bp, SEG, dk), vb[:np_].reshape(bp, SEG, dk), base=0, nq=1)
    k_s = jnp.concatenate([_bf(cache_k.reshape(bs, -1, dk)), kr[np_:].reshape(bs, parts * SEG, dk)], axis=1)
    v_s = jnp.concatenate([_bf(cache_v.reshape(bs, -1, dk)), vb[np_:].reshape(bs, parts * SEG, dk)], axis=1)
    o_s = _attention(q, k_s, v_s, base=bp, nq=parts)
    return jnp.concatenate([o_p, o_s], axis=0), kn[:np_], v[:np_]


HALO16 = 16


def _dn_proj_kernel(h_ref, hp_ref, hn_ref, win_ref, wg_ref, cw_ref, ga_ref, gb_ref,
                    q_ref, k_ref, v_ref, zs_ref, gd_ref, *, bp, parts, d, dk):
    is_start, is_end = _seg_flags(pl.program_id(0), bp, parts)
    h = h_ref[...]
    hcat = jnp.concatenate([hp_ref[...], h, hn_ref[...]], axis=0)
    n = hcat.shape[0]
    rid = lax.broadcasted_iota(jnp.int32, (n, d), 0)
    edge_prev = jnp.logical_and(is_start, rid == HALO16)
    edge_next = jnp.logical_and(is_end, rid == HALO16 + SEG - 1)
    outs = []
    for j in range(3):
        cols = slice(j * d, (j + 1) * d)
        x = _dot(hcat, win_ref[:, cols])
        prev = jnp.where(edge_prev, 0.0, pltpu.roll(x, 1, 0))
        nxt = jnp.where(edge_next, 0.0, pltpu.roll(x, n - 1, 0))
        c = cw_ref[0:1, cols] * prev + cw_ref[1:2, cols] * x + cw_ref[2:3, cols] * nxt
        c = c[HALO16:HALO16 + SEG]
        outs.append(c * _sigmoid(c))
    q, k, v = outs
    q_ref[...] = q * lax.rsqrt(_segsum(q * q, dk) + 1e-6) * (dk ** -0.5)
    k_ref[...] = k * lax.rsqrt(_segsum(k * k, dk) + 1e-6)
    v_ref[...] = v
    z = _dot(h, win_ref[:, 3 * d:])
    zs_ref[...] = z * _sigmoid(z)
    gt = _dot(h, wg_ref[...])
    lane = lax.broadcasted_iota(jnp.int32, gt.shape, 1) % LANES
    gd_ref[...] = jnp.where(lane < DN_HEADS, -jnp.exp(ga_ref[...]) * _softplus(gt + gb_ref[...]), _sigmoid(gt))


def _dn_proj(h, win, wg, conv_w, ga, gb, bp, parts, dk):
    m, d = h.shape
    hp, hn = _halo_specs(d, m, HALO16)
    return pl.pallas_call(
        functools.partial(_dn_proj_kernel, bp=bp, parts=parts, d=d, dk=dk),
        grid=(m // SEG,),
        in_specs=[_row_spec(d), hp, hn, _resident(win.shape), _resident(wg.shape), _const_spec((3, 3 * d)),
                  _const_spec((1, 2 * LANES)), _const_spec((1, 2 * LANES))],
        out_specs=[_row_spec(d)] * 4 + [_row_spec(2 * LANES)],
        out_shape=[jax.ShapeDtypeStruct((m, d), F32)] * 4 + [jax.ShapeDtypeStruct((m, 2 * LANES), F32)],
        compiler_params=_cparams("parallel"),
    )(h, h, h, win, wg, conv_w, ga, gb)


def _dn_scan_kernel(tbl_ref, *refs, nh, hd):
    ins, s0_ref, outs, sT_ref, s_scr = refs[:8], refs[8], refs[9:11], refs[11], refs[12]
    t = pl.program_id(0)
    n = tbl_ref[1, t]

    @pl.when(n == 0)
    def _():
        s_scr[...] = s0_ref[0]

    pk = _Packed()
    chains = []
    for e in range(2):
        q_ref, k_ref, v_ref, gd_ref = ins[4 * e:4 * e + 4]
        order = _order(1 - 2 * e)
        incl = order >= 0
        gd = gd_ref[...]
        cs = _dot_mask_exact(incl.astype(BF16), gd)
        tot = jnp.sum(gd, axis=0, keepdims=True)
        e_cs = jnp.exp(cs)
        e_end = jnp.exp(tot - cs)
        e_tot = jnp.exp(tot)
        cs_t = jnp.concatenate([cs, cs], axis=0).T
        for h in range(nh):
            sl = slice(h * hd, (h + 1) * hd)
            diff = jnp.broadcast_to(cs[:, h:h + 1], (CH, CH)) - jnp.broadcast_to(cs_t[h:h + 1, :CH], (CH, CH))
            chains.append(dict(
                e=e, sl=sl, incl=incl, strict=order > 0, beta=gd[:, nh + h:nh + h + 1],
                dec=jnp.where(incl, jnp.exp(jnp.where(incl, diff, 0.0)), 0.0),
                kh=k_ref[:, sl], qh=q_ref[:, sl], vh=v_ref[:, sl], e_cs=e_cs[:, h:h + 1],
                e_end=e_end[:, h:h + 1], e_tot=e_tot[:, h:h + 1], s_old=s_scr[e, h]))
    for c in chains:
        c["kb"] = c["kh"] * c["beta"]
        c["x"] = _dot_nt(jnp.concatenate([_bf(c["kb"]), _bf(c["qh"])], axis=0), _bf(c["kh"]))
        c["low"] = jnp.where(c["strict"], c["x"][:CH] * c["dec"], 0.0)
    packs = [chains[i:i + GRP] for i in range(0, len(chains), GRP)]
    t_inv = pk.tri_inverse([jnp.concatenate([c["low"] for c in p], axis=1) for p in packs])
    for p, tp in zip(packs, t_inv):
        for j, c in enumerate(p):
            c["t"] = _bf(tp[:, j * CH:(j + 1) * CH])
    for c in chains:
        vk = jnp.concatenate([_bf(c["vh"] * c["beta"]), _bf(c["kb"] * c["e_cs"])], axis=1)
        c["uw"] = _dot(c["t"], vk)
        c["sb"] = _bf(c["s_old"])
    for c in chains:
        c["v_new"] = _bf(c["uw"][:, :hd] - _dot(_bf(c["uw"][:, hd:]), c["sb"]))
    for c in chains:
        qk = _bf(jnp.where(c["incl"], c["x"][CH:] * c["dec"], 0.0))
        outs[c["e"]][:, c["sl"]] = _dot(_bf(c["qh"] * c["e_cs"]), c["sb"]) + _dot(qk, c["v_new"])
    for c in chains:
        h = c["sl"].start // hd
        s_scr[c["e"], h] = c["s_old"] * c["e_tot"] + _dot_tn(_bf(c["kh"] * c["e_end"]), c["v_new"])

    @pl.when(n == tbl_ref[2, t] - 1)
    def _():
        sT_ref[0] = s_scr[...]


def _dn_scan(q, k, v, gd, s0, tbl):
    m, d = q.shape
    nh, hd = s0.shape[2:4]
    ins = []
    for e in range(2):
        ins += [pl.BlockSpec((CH, d), _dir_row(e, 0))] * 3 + [pl.BlockSpec((CH, LANES), _dir_row(e, e))]
    st = pl.BlockSpec((1, 2, nh, hd, hd), lambda t, tbl: (tbl[3, t], 0, 0, 0, 0))
    return pl.pallas_call(
        functools.partial(_dn_scan_kernel, nh=nh, hd=hd),
        grid_spec=pltpu.PrefetchScalarGridSpec(
            num_scalar_prefetch=1, grid=(tbl.shape[1],),
            in_specs=ins + [st],
            out_specs=[pl.BlockSpec((CH, d), _dir_row(0, 0)), pl.BlockSpec((CH, d), _dir_row(1, 0)), st],
            scratch_shapes=[pltpu.VMEM((2, nh, hd, hd), F32)]),
        out_shape=[jax.ShapeDtypeStruct((m, d), F32)] * 2 + [jax.ShapeDtypeStruct(s0.shape, F32)],
        compiler_params=_cparams("arbitrary"),
    )(tbl, *([q, k, v, gd] * 2), s0)


def _dn_post_kernel(of_ref, ob_ref, zs_ref, ng_ref, y_ref, *, dv):
    o = of_ref[...] + ob_ref[...]
    y = o * lax.rsqrt(_segsum(o * o, dv) * (1.0 / dv) + 1e-6) * ng_ref[...]
    y_ref[...] = _bf(y * zs_ref[...])


def _dn_post(o_f, o_b, zs, ng, dv):
    m, d = o_f.shape
    return pl.pallas_call(
        functools.partial(_dn_post_kernel, dv=dv),
        grid=(m // SEG,),
        in_specs=[_row_spec(d)] * 3 + [_const_spec((1, d))],
        out_specs=_row_spec(d),
        out_shape=jax.ShapeDtypeStruct((m, d), BF16),
        compiler_params=_cparams("parallel"),
    )(o_f, o_b, zs, ng)


def _deltanet_mixer(h, state, p, dims):
    bp, bs, parts = dims
    m, d = h.shape
    dk = d // DN_HEADS
    q, k, v, zs, gd = _dn_proj(h, p["win"], p["wg"], p["conv"], p["ga"], p["gb"], bp, parts, dk)
    s0 = jnp.concatenate([jnp.zeros((bp,) + state.shape[1:], F32), state], axis=0)
    o_f, o_b, s_fin = _dn_scan(q, k, v, gd, s0, _scan_table(bp, bs, parts))
    y = _dn_post(o_f, o_b, zs, jnp.tile(p["ng"], DN_HEADS).reshape(1, d), dk)
    return y, s_fin[:bp]


def _gate_columns(w_gate):
    dd = w_gate.shape[0]
    pad = jnp.zeros((dd, LANES - 2 * DN_HEADS), w_gate.dtype)
    cols = []
    for e in range(2):
        cols += [w_gate[:, e * DN_HEADS:(e + 1) * DN_HEADS],
                 w_gate[:, (2 + e) * DN_HEADS:(3 + e) * DN_HEADS], pad]
    return jnp.concatenate(cols, axis=1)


def _gate_lanes(x):
    pad = jnp.zeros((LANES - DN_HEADS,), x.dtype)
    return jnp.concatenate([x[0], pad, x[1], pad]).reshape(1, 2 * LANES)


def kernel(x_prompt, x_sample, state_rwkv, cache_k, cache_v, state_delta, c, c_ctx, w_mod, b_mod, ln_g, ln_b, w_fc1, w_fc2, rw_mu, rw_wrkv, rw_w0, rw_w1, rw_w2, rw_a0, rw_a1, rw_a2, rw_g1, rw_g2, rw_kk, rw_ka, rw_rk, rw_gn_g, rw_gn_b, rw_wo, at_wqkv, at_qn, at_kn, at_wo, dn_win, dn_conv, dn_alog, dn_dtb, dn_ng, dn_wo):
    bp, tp, d = x_prompt.shape
    bs, ts, _ = x_sample.shape
    assert tp == SEG and ts % SEG == 0 and bs + 1 <= SUBLANES
    parts = ts // SEG
    dims = (bp, bs, parts)
    depth = w_mod.shape[0]
    alpha = (2.0 * depth) ** 0.25

    x = jnp.concatenate([x_prompt.reshape(bp * tp, d), x_sample.reshape(bs * ts, d)], axis=0)
    cond = jnp.concatenate([c, c_ctx[None, :], jnp.zeros((SUBLANES - bs - 1, d), F32)], axis=0)
    mods = _modulation(cond, w_mod, b_mod)
    seg_row = jnp.array([bs] * bp + [b for b in range(bs) for _ in range(parts)], jnp.int32)
    modseg = [mods[l][seg_row].reshape(bp + bs * parts, N_MOD, d) for l in range(depth)]

    def mixer_in_dtype(l):
        return F32 if l % 3 == 0 else BF16

    new_rwkv, new_k, new_v, new_delta = [], [], [], []
    h = _premod(x, modseg[0], mixer_in_dtype(0))
    for l in range(depth):
        kind, j = l % 3, l // 3
        if kind == 0:
            z64 = jnp.zeros((RW_HEAD, d), F32)
            prm = dict(
                mu=rw_mu[j], wr=_bf(rw_wrkv[j, 0]), wk=_bf(rw_wrkv[j, 1]), wv=_bf(rw_wrkv[j, 2]),
                w1=_bf(jnp.concatenate([rw_w1[j, 0], rw_w1[j, 1]], axis=1)),
                w2=_bf(jnp.concatenate([jnp.concatenate([rw_w2[j, 0], z64], axis=1),
                                        jnp.concatenate([z64, rw_w2[j, 1]], axis=1)], axis=0)),
                a1=_bf(jnp.concatenate([rw_a1[j, 0], rw_a1[j, 1]], axis=1)),
                a2=_bf(jnp.concatenate([jnp.concatenate([rw_a2[j, 0], z64], axis=1),
                                        jnp.concatenate([z64, rw_a2[j, 1]], axis=1)], axis=0)),
                g1=_bf(rw_g1[j]), g2=_bf(rw_g2[j]), w0=rw_w0[j], a0=rw_a0[j],
                kk=rw_kk[j], ka=rw_ka[j], rk=rw_rk[j].reshape(-1), gn_g=rw_gn_g[j], gn_b=rw_gn_b[j],
                wo=_bf(rw_wo[j]))
            y, s_new = _rwkv_mixer(h, state_rwkv[:, j], prm, dims)
            new_rwkv.append(s_new)
        elif kind == 1:
            prm = dict(wqkv=_bf(at_wqkv[j]), qn=at_qn[j], kn=at_kn[j], wo=_bf(at_wo[j]))
            y, kp, vp = _attention_mixer(h, cache_k[:, j], cache_v[:, j], prm, dims)
            new_k.append(kp.reshape(bp, tp, -1, AT_HEAD))
            new_v.append(vp.reshape(bp, tp, -1, AT_HEAD))
        else:
            prm = dict(win=_bf(dn_win[j, :, :4 * d]), wg=_bf(_gate_columns(dn_win[j, :, 4 * d:])),
                       conv=dn_conv[j], ga=_gate_lanes(dn_alog[j]), gb=_gate_lanes(dn_dtb[j]),
                       ng=dn_ng[j], wo=_bf(dn_wo[j]))
            y, s_new = _deltanet_mixer(h, state_delta[:, j], prm, dims)
            new_delta.append(s_new)
        x, h = _sublayer_ln(_mm_ln_kernel, y, [prm["wo"]], x, modseg[l], modseg[l], ln_g[l, 0], ln_b[l, 0],
                            alpha=alpha, gate_idx=2, sh_idx=3, h_dtype=BF16)
        nxt = min(l + 1, depth - 1)
        x, h = _sublayer_ln(_mlp_ln_kernel, h, [_bf(w_fc1[l]), _bf(w_fc2[l])], x, modseg[l], modseg[nxt],
                            ln_g[l, 1], ln_b[l, 1], alpha=alpha, gate_idx=5, sh_idx=0,
                            h_dtype=mixer_in_dtype(l + 1) if l + 1 < depth else None, fc=d)
    y_prompt = x[:bp * tp].reshape(bp, tp, d)
    y_sample = x[bp * tp:].reshape(bs, ts, d)
    return (y_prompt, y_sample, jnp.stack(new_rwkv, axis=1), jnp.stack(new_k, axis=1),
            jnp.stack(new_v, axis=1), jnp.stack(new_delta, axis=1))
```

```python
import functools

import jax
import jax.numpy as jnp
import numpy as np
from jax import lax
from jax.experimental import pallas as pl
from jax.experimental.pallas import tpu as pltpu

F32 = jnp.float32
BF16 = jnp.bfloat16

SEG = 256
CH = 64
GRP = 4
LANES = 128
SUBLANES = 8
VMEM_LIMIT = 48 * 1024 * 1024

N_MOD = 6
LN_EPS = 1e-5
RW_HEAD = 64
RW_GN_EPS = 64e-5
AT_HEAD = 64
AT_GROUP = 4
GRID_W = 64
ROPE_THETA = 10000.0
DN_HEADS = 8


def _cparams(*sem):
    return pltpu.CompilerParams(dimension_semantics=sem, vmem_limit_bytes=VMEM_LIMIT)


def _dot(a, b):
    return jnp.dot(a, b, preferred_element_type=F32)


def _dot_nt(a, b):
    return lax.dot_general(a, b, (((1,), (1,)), ((), ())), preferred_element_type=F32)


def _dot_tn(a, b):
    return lax.dot_general(a, b, (((0,), (0,)), ((), ())), preferred_element_type=F32)


def _bf(x):
    return x.astype(BF16)


def _split3(x):
    hi = _bf(x)
    r1 = x - hi.astype(F32)
    mid = _bf(r1)
    lo = _bf(r1 - mid.astype(F32))
    return hi, mid, lo


def _dot_mask_exact(mask_bf, x):
    hi, mid, lo = _split3(x)
    return _dot(mask_bf, hi) + _dot(mask_bf, mid) + _dot(mask_bf, lo)


def _segsum(x, seg):
    w = x.shape[1]
    li = lax.broadcasted_iota(jnp.int32, (LANES, LANES), 0) // seg
    lj = lax.broadcasted_iota(jnp.int32, (LANES, LANES), 1) // seg
    bd = (li == lj).astype(BF16)
    outs = []
    for j in range(w // LANES):
        xs = x[:, LANES * j:LANES * (j + 1)]
        hi = _bf(xs)
        lo = _bf(xs - hi.astype(F32))
        outs.append(_dot(hi, bd) + _dot(lo, bd))
    return outs[0] if len(outs) == 1 else jnp.concatenate(outs, axis=1)


def _softplus(x):
    return jnp.maximum(x, 0.0) + jnp.log1p(jnp.exp(-jnp.abs(x)))


def _sigmoid(x):
    return 1.0 / (1.0 + jnp.exp(-x))


def _seg_flags(i, bp, parts):
    j = (i - bp) % parts
    is_ctx = i < bp
    return jnp.logical_or(is_ctx, j == 0), jnp.logical_or(is_ctx, j == parts - 1)


def _shift_prev(x, xp8, is_start):
    rolled = pltpu.roll(x, 1, 0)
    row0 = jnp.where(is_start, 0.0, xp8[SUBLANES - 1:SUBLANES, :])
    rid = lax.broadcasted_iota(jnp.int32, x.shape, 0)
    return jnp.where(rid == 0, row0, rolled)


def _shift_next(x, xn8, is_end):
    n = x.shape[0]
    rolled = pltpu.roll(x, n - 1, 0)
    rowl = jnp.where(is_end, 0.0, xn8[0:1, :])
    rid = lax.broadcasted_iota(jnp.int32, x.shape, 0)
    return jnp.where(rid == n - 1, rowl, rolled)


def _row_spec(w, col=0):
    return pl.BlockSpec((SEG, w), lambda i: (i, col))


def _halo_specs(w, m_rows, rows=SUBLANES):
    per = SEG // rows
    last = m_rows // rows - 1
    return (pl.BlockSpec((rows, w), lambda i: (jnp.maximum(i * per - 1, 0), 0)),
            pl.BlockSpec((rows, w), lambda i: (jnp.minimum((i + 1) * per, last), 0)))


def _const_spec(shape):
    nd = len(shape)
    return pl.BlockSpec(shape, lambda i: (0,) * nd)


def _segvec_spec(k, w):
    return pl.BlockSpec((1, k, w), lambda i: (i, 0, 0))


def _mod_kernel(c_ref, w_ref, b_ref, o_ref):
    c = c_ref[...]
    s = c * _sigmoid(c)
    o_ref[0] = _dot(_bf(s), _bf(w_ref[0])) + b_ref[0]


def _modulation(cond, w_mod, b_mod, tn=1536):
    depth, d, n = w_mod.shape
    return pl.pallas_call(
        _mod_kernel,
        grid=(depth, n // tn),
        in_specs=[pl.BlockSpec((SUBLANES, d), lambda l, j: (0, 0)),
                  pl.BlockSpec((1, d, tn), lambda l, j: (l, 0, j)),
                  pl.BlockSpec((1, 1, tn), lambda l, j: (l, 0, j))],
        out_specs=pl.BlockSpec((1, SUBLANES, tn), lambda l, j: (l, 0, j)),
        out_shape=jax.ShapeDtypeStruct((depth, SUBLANES, n), F32),
        compiler_params=_cparams("parallel", "parallel"),
    )(cond, w_mod, b_mod.reshape(depth, 1, n))


def _premod_kernel(x_ref, m_ref, h_ref):
    h = x_ref[...] * (1.0 + m_ref[0, 1:2, :]) + m_ref[0, 0:1, :]
    h_ref[...] = h.astype(h_ref.dtype)


def _premod(x, modseg, out_dtype):
    m, d = x.shape
    return pl.pallas_call(
        _premod_kernel,
        grid=(m // SEG,),
        in_specs=[_row_spec(d), _segvec_spec(N_MOD, d)],
        out_specs=_row_spec(d),
        out_shape=jax.ShapeDtypeStruct((m, d), out_dtype),
        compiler_params=_cparams("parallel"),
    )(x, modseg)


def _seg_rows(m_ref, idx, spt):
    d = m_ref.shape[2]
    rows = [jnp.broadcast_to(m_ref[s, idx:idx + 1, :], (SEG, d)) for s in range(spt)]
    return rows[0] if spt == 1 else jnp.concatenate(rows, axis=0)


def _ln_epilogue(x, delta, m_ref, mn_ref, g_ref, b_ref, xo_ref, ho_ref, *, alpha, gate_idx, sh_idx, spt):
    y = alpha * x + _seg_rows(m_ref, gate_idx, spt) * delta
    mu = jnp.mean(y, axis=-1, keepdims=True)
    yc = y - mu
    var = jnp.mean(yc * yc, axis=-1, keepdims=True)
    xn = yc * lax.rsqrt(var + LN_EPS) * g_ref[...] + b_ref[...]
    xo_ref[...] = xn
    if ho_ref is not None:
        h = xn * (1.0 + _seg_rows(mn_ref, sh_idx + 1, spt)) + _seg_rows(mn_ref, sh_idx, spt)
        ho_ref[...] = h.astype(ho_ref.dtype)


def _mm_ln_kernel(y_ref, w_ref, x_ref, m_ref, mn_ref, g_ref, b_ref, xo_ref, *rest, **kw):
    delta = _dot(y_ref[...], w_ref[...])
    _ln_epilogue(x_ref[...], delta, m_ref, mn_ref, g_ref, b_ref, xo_ref, rest[0] if rest else None, **kw)


def _rw_out_ln_kernel(of_ref, ob_ref, bonus_ref, gate_ref, gng_ref, gnb_ref, w_ref,
                      x_ref, m_ref, mn_ref, g_ref, b_ref, xo_ref, *rest, **kw):
    o = of_ref[...] + ob_ref[...]
    oc = o - _segsum(o, RW_HEAD) * (1.0 / RW_HEAD)
    var = _segsum(oc * oc, RW_HEAD) * (1.0 / RW_HEAD)
    y = oc * lax.rsqrt(var + RW_GN_EPS) * gng_ref[...] + gnb_ref[...]
    delta = _dot(_bf((y + bonus_ref[...]) * gate_ref[...]), w_ref[...])
    _ln_epilogue(x_ref[...], delta, m_ref, mn_ref, g_ref, b_ref, xo_ref, rest[0] if rest else None, **kw)


def _dn_out_ln_kernel(of_ref, ob_ref, zs_ref, ng_ref, w_ref,
                      x_ref, m_ref, mn_ref, g_ref, b_ref, xo_ref, *rest, dv, **kw):
    o = of_ref[...] + ob_ref[...]
    y = o * lax.rsqrt(_segsum(o * o, dv) * (1.0 / dv) + 1e-6) * ng_ref[...]
    delta = _dot(_bf(y * zs_ref[...]), w_ref[...])
    _ln_epilogue(x_ref[...], delta, m_ref, mn_ref, g_ref, b_ref, xo_ref, rest[0] if rest else None, **kw)


def _mlp_ln_kernel(h_ref, w1_ref, w2_ref, x_ref, m_ref, mn_ref, g_ref, b_ref, xo_ref, *rest, fc, **kw):
    hb = h_ref[...]
    delta = jnp.zeros(x_ref.shape, F32)
    for j in range(w1_ref.shape[1] // fc):
        u = jnp.square(jnp.maximum(_dot(hb, w1_ref[:, j * fc:(j + 1) * fc]), 0.0))
        delta = delta + _dot(_bf(u), w2_ref[j * fc:(j + 1) * fc, :])
    _ln_epilogue(x_ref[...], delta, m_ref, mn_ref, g_ref, b_ref, xo_ref, rest[0] if rest else None, **kw)


def _resident(shape):
    nd = len(shape)
    return pl.BlockSpec(shape, lambda i: (0,) * nd, pipeline_mode=pl.Buffered(1))


def _sublayer_ln(kern, rows, vecs, weights, x, modseg, modseg_next, g, b, *, alpha, gate_idx, sh_idx,
                 h_dtype, **static):
    m, d = x.shape
    nseg = modseg.shape[0]
    spt = 2 if nseg % 2 == 0 else 1
    tm = spt * SEG
    row = lambda w: pl.BlockSpec((tm, w), lambda i: (i, 0))
    segv = pl.BlockSpec((spt, N_MOD, d), lambda i: (i, 0, 0))
    out_shape = [jax.ShapeDtypeStruct((m, d), F32)]
    out_specs = [row(d)]
    if h_dtype is not None:
        out_shape.append(jax.ShapeDtypeStruct((m, d), h_dtype))
        out_specs.append(row(d))
    res = pl.pallas_call(
        functools.partial(kern, alpha=alpha, gate_idx=gate_idx, sh_idx=sh_idx, spt=spt, **static),
        grid=(m // tm,),
        in_specs=[row(r.shape[1]) for r in rows] + [_const_spec(v.shape) for v in vecs]
        + [_resident(w.shape) for w in weights]
        + [row(d), segv, segv, _const_spec((1, d)), _const_spec((1, d))],
        out_specs=out_specs,
        out_shape=out_shape,
        compiler_params=_cparams("parallel"),
    )(*rows, *vecs, *weights, x, modseg, modseg_next, g.reshape(1, d), b.reshape(1, d))
    return (res[0], res[1]) if h_dtype is not None else (res[0], None)


class _Packed:
    def __init__(self):
        gw = GRP * CH
        self.ti = lax.broadcasted_iota(jnp.int32, (CH, gw), 0)
        lane = lax.broadcasted_iota(jnp.int32, (CH, gw), 1)
        self.si = lane % CH
        self.grp = lane // CH
        self.eye = (self.ti == self.si).astype(F32)
        self.merge = []
        b = 1
        while b < CH:
            self.merge.append(jnp.logical_and(self.ti // (2 * b) == self.si // (2 * b),
                                              self.ti // b != self.si // b))
            b *= 2

    def bd(self, x):
        return jnp.concatenate([_bf(jnp.where(self.grp == j, x, 0.0)) for j in range(GRP)], axis=0)

    def tri_inverse(self, lows):
        ts = [self.eye - jnp.where(self.merge[0], low, 0.0) for low in lows]
        for mk in self.merge[1:]:
            xs = [_dot(_bf(jnp.where(mk, low, 0.0)), self.bd(t)) for low, t in zip(lows, ts)]
            ts = [t - _dot(_bf(t), self.bd(x)) for t, x in zip(ts, xs)]
        return ts


def _order(sign):
    return (lax.broadcasted_iota(jnp.int32, (CH, CH), 0) - lax.broadcasted_iota(jnp.int32, (CH, CH), 1)) * sign


def _scan_table(bp, bs, parts):
    ncp = SEG // CH
    ncs = parts * ncp
    steps = [(b * ncp, n, ncp, b) for b in range(bp) for n in range(ncp)]
    steps += [(bp * ncp + b * ncs, n, ncs, bp + b) for b in range(bs) for n in range(ncs)]
    return jnp.asarray(np.array(steps, np.int32).T)


def _dir_row(e, col):
    def index(t, tbl):
        n = tbl[1, t]
        return tbl[0, t] + n + e * (tbl[2, t] - 1 - 2 * n), col
    return index


def _rw_proj_kernel(h_ref, hp_ref, hn_ref, mu_ref, wr_ref, wk_ref, wv_ref, w1_ref, w2_ref, a1_ref, a2_ref,
                    g1_ref, g2_ref, w0_ref, a0_ref, kk_ref, ka_ref, rk_ref,
                    r_ref, v_ref, g_ref, ld_ref, be_ref, kd_ref, alpha_ref, bonus_ref, *, bp, parts):
    d = h_ref.shape[1]
    is_start, is_end = _seg_flags(pl.program_id(0), bp, parts)
    h = h_ref[...]
    xx = 0.5 * (_shift_prev(h, hp_ref[...], is_start) + _shift_next(h, hn_ref[...], is_end)) - h

    def mix(j):
        return _bf(h + xx * mu_ref[j:j + 1, :])

    r = _dot(mix(0), wr_ref[...])
    k = _dot(mix(2), wk_ref[...])
    v = _dot(mix(3), wv_ref[...])
    wl = _dot(_bf(jnp.tanh(_dot(mix(1), w1_ref[...]))), w2_ref[...])
    al = _dot(_bf(_dot(mix(4), a1_ref[...])), a2_ref[...])
    g_ref[...] = _dot(_bf(_sigmoid(_dot(mix(5), g1_ref[...]))), g2_ref[...])
    r_ref[...] = r
    v_ref[...] = v
    kkn = k * kk_ref[...]
    kk = kkn * lax.rsqrt(_segsum(kkn * kkn, RW_HEAD) + 1e-6)
    alpha_ref[...] = -kk
    bon = jnp.zeros_like(r)
    for e in range(2):
        cols = slice(e * d, (e + 1) * d)
        ld_ref[:, cols] = -float(np.exp(-0.5)) * _sigmoid(w0_ref[e:e + 1, :] + wl[:, cols])
        a = _sigmoid(a0_ref[e:e + 1, :] + al[:, cols])
        be_ref[:, cols] = kk * a
        kd = k * (1.0 + (a - 1.0) * ka_ref[...])
        kd_ref[:, cols] = kd
        bon = bon + r * kd * rk_ref[...]
    bonus_ref[...] = _segsum(bon, RW_HEAD) * v


def _rw_proj(h, p, bp, parts):
    m, d = h.shape
    hp, hn = _halo_specs(d, m)
    weights = [p[n] for n in ("wr", "wk", "wv", "w1", "w2", "a1", "a2", "g1", "g2")]
    vecs = [p["w0"], p["a0"], p["kk"].reshape(1, d), p["ka"].reshape(1, d), p["rk"].reshape(1, d)]
    return pl.pallas_call(
        functools.partial(_rw_proj_kernel, bp=bp, parts=parts),
        grid=(m // SEG,),
        in_specs=[_row_spec(d), hp, hn, _const_spec((6, d))] + [_resident(w.shape) for w in weights]
        + [_const_spec(x.shape) for x in vecs],
        out_specs=[_row_spec(d)] * 3 + [_row_spec(2 * d)] * 3 + [_row_spec(d)] * 2,
        out_shape=[jax.ShapeDtypeStruct((m, d), F32)] * 3 + [jax.ShapeDtypeStruct((m, 2 * d), F32)] * 3
        + [jax.ShapeDtypeStruct((m, d), F32)] * 2,
        compiler_params=_cparams("parallel"),
    )(h, h, h, p["mu"], *weights, *vecs)


def _rw_scan_kernel(tbl_ref, *refs, ng, hd):
    ins, s0_ref, outs, sT_ref, s_scr = refs[:12], refs[12], refs[13:15], refs[15], refs[16]
    t = pl.program_id(0)
    n = tbl_ref[1, t]
    assert hd == CH
    gw = GRP * hd

    @pl.when(n == 0)
    def _():
        for e in range(2):
            for g in range(ng):
                rows = []
                for j in range(GRP):
                    parts_ = [jnp.zeros((hd, j * hd), F32)] if j else []
                    parts_.append(s0_ref[0, e, g * GRP + j])
                    if j < GRP - 1:
                        parts_.append(jnp.zeros((hd, (GRP - 1 - j) * hd), F32))
                    rows.append(jnp.concatenate(parts_, axis=1))
                s_scr[e, g] = jnp.concatenate(rows, axis=0)

    pk = _Packed()
    bd = pk.bd
    own = (lax.broadcasted_iota(jnp.int32, (gw, gw), 0) // hd
           == lax.broadcasted_iota(jnp.int32, (gw, gw), 1) // hd)
    cols = lambda g: slice(g * gw, (g + 1) * gw)
    chains = []
    for e in range(2):
        r_ref, ld_ref, al_ref, be_ref, kd_ref, v_ref = ins[6 * e:6 * e + 6]
        sign = 1 - 2 * e
        ahead = (pk.ti - pk.si) * sign
        ld = ld_ref[...]
        cs = _dot_mask_exact((_order(sign) >= 0).astype(BF16), ld)
        tot = jnp.sum(ld, axis=0, keepdims=True)
        g_end = jnp.exp(tot - cs)
        g_inv = jnp.exp(-cs)
        be = be_ref[...]
        kd = kd_ref[...]
        full = dict(ab=al_ref[...] * jnp.exp(cs - ld), qb=r_ref[...] * jnp.exp(cs), bt=be * g_inv,
                    kt=kd * g_inv, b_end=be * g_end, k_end=kd * g_end, v=v_ref[...], g_tot=jnp.exp(tot))
        for g in range(ng):
            c = {k: x[:, cols(g)] for k, x in full.items()}
            c.update(e=e, g=g, incl=ahead >= 0, strict=ahead > 0, s_old=s_scr[e, g])
            chains.append(c)
    for c in chains:
        c["sb"] = _bf(c["s_old"])
        c["abb"] = _bf(c["ab"])
        c["qbb"] = _bf(c["qb"])
        c["vbd"] = bd(c["v"])
        aq = jnp.concatenate([c["abb"], c["qbb"]], axis=0)
        c["xb"] = _dot_nt(aq, bd(c["bt"]))
        c["xk"] = _dot_nt(aq, bd(c["kt"]))
        c["sq"] = _dot_nt(aq, c["sb"])
    for c in chains:
        a_k = jnp.concatenate([jnp.where(c["strict"], c["xk"][:CH], 0.0),
                               jnp.where(c["incl"], c["xk"][CH:], 0.0)], axis=0)
        c["av"] = _dot(_bf(a_k), c["vbd"])
    t_inv = pk.tri_inverse([jnp.where(c["strict"], -c["xb"][:CH], 0.0) for c in chains])
    for c, tc in zip(chains, t_inv):
        c["u"] = _dot(_bf(tc), bd(c["sq"][:CH] + c["av"][:CH]))
    for c in chains:
        a_qb = _bf(jnp.where(c["incl"], c["xb"][CH:], 0.0))
        outs[c["e"]][:, cols(c["g"])] = c["sq"][CH:] + c["av"][CH:] + _dot(a_qb, bd(c["u"]))
    for c in chains:
        uv = jnp.concatenate([_bf(c["u"]), _bf(c["v"])], axis=0)
        bk = jnp.concatenate([_bf(c["b_end"]), _bf(c["k_end"])], axis=0)
        s_scr[c["e"], c["g"]] = c["s_old"] * c["g_tot"] + jnp.where(own, _dot_tn(uv, bk), 0.0)

    @pl.when(n == tbl_ref[2, t] - 1)
    def _():
        for e in range(2):
            for g in range(ng):
                for j in range(GRP):
                    sT_ref[0, e, g * GRP + j] = s_scr[e, g, j * hd:(j + 1) * hd, j * hd:(j + 1) * hd]


def _rw_scan(r, ld, alpha, be, kd, v, s0, tbl):
    m, d = r.shape
    nh, hd = s0.shape[2:4]
    ng, gw = nh // GRP, GRP * hd
    ins = []
    for e in range(2):
        ins += [pl.BlockSpec((CH, d), _dir_row(e, c)) for c in (0, e, 0, e, e, 0)]
    st = pl.BlockSpec((1, 2, nh, hd, hd), lambda t, tbl: (tbl[3, t], 0, 0, 0, 0))
    return pl.pallas_call(
        functools.partial(_rw_scan_kernel, ng=ng, hd=hd),
        grid_spec=pltpu.PrefetchScalarGridSpec(
            num_scalar_prefetch=1, grid=(tbl.shape[1],),
            in_specs=ins + [st],
            out_specs=[pl.BlockSpec((CH, d), _dir_row(0, 0)), pl.BlockSpec((CH, d), _dir_row(1, 0)), st],
            scratch_shapes=[pltpu.VMEM((2, ng, gw, gw), F32)]),
        out_shape=[jax.ShapeDtypeStruct((m, d), F32)] * 2 + [jax.ShapeDtypeStruct(s0.shape, F32)],
        compiler_params=_cparams("arbitrary"),
    )(tbl, *([r, ld, alpha, be, kd, v] * 2), s0)


def _rwkv_mixer(h, state, p, dims):
    bp, bs, parts = dims
    m, d = h.shape
    r, v, g, ld, be, kd, alpha, bonus = _rw_proj(h, p, bp, parts)
    s0 = jnp.concatenate([jnp.zeros((bp,) + state.shape[1:], F32), state], axis=0)
    o_f, o_b, s_fin = _rw_scan(r, ld, alpha, be, kd, v, s0, _scan_table(bp, bs, parts))
    return (o_f, o_b, bonus, g), s_fin[:bp]


def _rope(y, cos, sin):
    w = y.shape[1]
    up = pltpu.roll(y, w - AT_HEAD // 4, 1)
    dn = pltpu.roll(y, AT_HEAD // 4, 1)
    lane = lax.broadcasted_iota(jnp.int32, y.shape, 1)
    first = (lane % (AT_HEAD // 2)) < (AT_HEAD // 4)
    return y * cos + jnp.where(first, up, dn) * sin


def _at_proj_kernel(h_ref, w_ref, cos_ref, sin_ref, qg_ref, kg_ref, q_ref, kr_ref, kn_ref, v_ref, vb_ref,
                    *, dq, dk):
    q = _dot(h_ref[...], w_ref[:, :dq])
    k = _dot(h_ref[...], w_ref[:, dq:dq + dk])
    v = _dot(h_ref[...], w_ref[:, dq + dk:])
    qn = q * lax.rsqrt(_segsum(q * q, AT_HEAD) * (1.0 / AT_HEAD) + 1e-6) * qg_ref[...]
    kn = k * lax.rsqrt(_segsum(k * k, AT_HEAD) * (1.0 / AT_HEAD) + 1e-6) * kg_ref[...]
    kn_ref[...] = kn
    v_ref[...] = v
    vb_ref[...] = _bf(v)
    q_ref[...] = _bf(_rope(qn, cos_ref[...], sin_ref[...]))
    kr_ref[...] = _bf(_rope(kn, cos_ref[:, :dk], sin_ref[:, :dk]))


def _at_proj(h, wqkv, cos_t, sin_t, qg, kg, bp, parts, dq, dk):
    m, d = h.shape
    tab = pl.BlockSpec((SEG, dq), lambda i: (jnp.where(i < bp, 0, 1 + (i - bp) % parts), 0))
    return pl.pallas_call(
        functools.partial(_at_proj_kernel, dq=dq, dk=dk),
        grid=(m // SEG,),
        in_specs=[_row_spec(d), _resident(wqkv.shape), tab, tab, _const_spec((1, dq)), _const_spec((1, dk))],
        out_specs=[_row_spec(dq)] + [_row_spec(dk)] * 4,
        out_shape=[jax.ShapeDtypeStruct((m, dq), BF16), jax.ShapeDtypeStruct((m, dk), BF16),
                   jax.ShapeDtypeStruct((m, dk), F32), jax.ShapeDtypeStruct((m, dk), F32),
                   jax.ShapeDtypeStruct((m, dk), BF16)],
        compiler_params=_cparams("parallel"),
    )(h, wqkv, cos_t, sin_t, qg, kg)


def _attn_kernel(q_ref, k_ref, v_ref, o_ref, *, scale):
    nkv = k_ref.shape[2] // AT_HEAD
    for g in range(nkv):
        kg = k_ref[0, :, g * AT_HEAD:(g + 1) * AT_HEAD]
        vg = v_ref[0, :, g * AT_HEAD:(g + 1) * AT_HEAD]
        for j in range(AT_GROUP):
            sl = slice((g * AT_GROUP + j) * AT_HEAD, (g * AT_GROUP + j + 1) * AT_HEAD)
            s = _dot_nt(q_ref[:, sl], kg) * scale
            p = jnp.exp(s - jnp.max(s, axis=-1, keepdims=True))
            l = jnp.sum(p, axis=-1, keepdims=True)
            o_ref[:, sl] = (_dot(_bf(p), vg) / l).astype(o_ref.dtype)


def _attention(q, k, v, *, base, nq):
    dq = q.shape[1]
    b, tk, dk = k.shape
    kv = pl.BlockSpec((1, tk, dk), lambda bb, j: (bb, 0, 0))
    return pl.pallas_call(
        functools.partial(_attn_kernel, scale=AT_HEAD ** -0.5),
        grid=(b, nq),
        in_specs=[pl.BlockSpec((SEG, dq), lambda bb, j: (base + bb * nq + j, 0)), kv, kv],
        out_specs=pl.BlockSpec((SEG, dq), lambda bb, j: (bb * nq + j, 0)),
        out_shape=jax.ShapeDtypeStruct((b * nq * SEG, dq), BF16),
        compiler_params=_cparams("parallel", "parallel"),
    )(q, k, v)


def _rope_tables(t, heads):
    rows = t // GRID_W
    row = jnp.repeat(jnp.arange(rows), GRID_W)
    col = jnp.tile(jnp.arange(GRID_W), rows)
    n_freq = AT_HEAD // 4
    inv_freq = ROPE_THETA ** (-jnp.arange(n_freq, dtype=F32) / n_freq)
    ang = jnp.stack([row, col], axis=-1).astype(F32)[:, :, None] * inv_freq
    cos = jnp.repeat(jnp.cos(ang)[:, :, None, :], 2, axis=2).reshape(t, AT_HEAD)
    sin = jnp.sin(ang)
    sin = jnp.stack([-sin, sin], axis=2).reshape(t, AT_HEAD)
    ident = (jnp.ones((SEG, AT_HEAD), F32), jnp.zeros((SEG, AT_HEAD), F32))
    cos = jnp.tile(jnp.concatenate([ident[0], cos], axis=0), (1, heads))
    sin = jnp.tile(jnp.concatenate([ident[1], sin], axis=0), (1, heads))
    return cos, sin


def _attention_mixer(h, cache_k, cache_v, p, dims):
    bp, bs, parts = dims
    m, d = h.shape
    dk = cache_k.shape[-1] * cache_k.shape[-2]
    heads = d // AT_HEAD
    cos_t, sin_t = _rope_tables(parts * SEG, heads)
    q, kr, kn, v, vb = _at_proj(h, p["wqkv"], cos_t, sin_t, jnp.tile(p["qn"], heads).reshape(1, d),
                                jnp.tile(p["kn"], dk // AT_HEAD).reshape(1, dk), bp, parts, d, dk)
    np_ = bp * SEG
    o_p = _attention(q, kr[:np_].reshape(bp, SEG, dk), vb[:np_].reshape(bp, SEG, dk), base=0, nq=1)
    k_s = jnp.concatenate([_bf(cache_k.reshape(bs, -1, dk)), kr[np_:].reshape(bs, parts * SEG, dk)], axis=1)
    v_s = jnp.concatenate([_bf(cache_v.reshape(bs, -1, dk)), vb[np_:].reshape(bs, parts * SEG, dk)], axis=1)
    o_s = _attention(q, k_s, v_s, base=bp, nq=parts)
    return jnp.concatenate([o_p, o_s], axis=0), kn[:np_], v[:np_]


HALO16 = 16


def _dn_proj_kernel(h_ref, hp_ref, hn_ref, win_ref, wg_ref, cw_ref, ga_ref, gb_ref,
                    q_ref, k_ref, v_ref, zs_ref, gd_ref, *, bp, parts, d, dk):
    is_start, is_end = _seg_flags(pl.program_id(0), bp, parts)
    h = h_ref[...]
    hcat = jnp.concatenate([hp_ref[...], h, hn_ref[...]], axis=0)
    n = hcat.shape[0]
    rid = lax.broadcasted_iota(jnp.int32, (n, d), 0)
    edge_prev = jnp.logical_and(is_start, rid == HALO16)
    edge_next = jnp.logical_and(is_end, rid == HALO16 + SEG - 1)
    outs = []
    for j in range(3):
        cols = slice(j * d, (j + 1) * d)
        x = _dot(hcat, win_ref[:, cols])
        prev = jnp.where(edge_prev, 0.0, pltpu.roll(x, 1, 0))
        nxt = jnp.where(edge_next, 0.0, pltpu.roll(x, n - 1, 0))
        c = cw_ref[0:1, cols] * prev + cw_ref[1:2, cols] * x + cw_ref[2:3, cols] * nxt
        c = c[HALO16:HALO16 + SEG]
        outs.append(c * _sigmoid(c))
    q, k, v = outs
    q_ref[...] = q * lax.rsqrt(_segsum(q * q, dk) + 1e-6) * (dk ** -0.5)
    k_ref[...] = k * lax.rsqrt(_segsum(k * k, dk) + 1e-6)
    v_ref[...] = v
    z = _dot(h, win_ref[:, 3 * d:])
    zs_ref[...] = z * _sigmoid(z)
    gt = _dot(h, wg_ref[...])
    lane = lax.broadcasted_iota(jnp.int32, gt.shape, 1) % LANES
    gd_ref[...] = jnp.where(lane < DN_HEADS, -jnp.exp(ga_ref[...]) * _softplus(gt + gb_ref[...]), _sigmoid(gt))


def _dn_proj(h, win, wg, conv_w, ga, gb, bp, parts, dk):
    m, d = h.shape
    hp, hn = _halo_specs(d, m, HALO16)
    return pl.pallas_call(
        functools.partial(_dn_proj_kernel, bp=bp, parts=parts, d=d, dk=dk),
        grid=(m // SEG,),
        in_specs=[_row_spec(d), hp, hn, _resident(win.shape), _resident(wg.shape), _const_spec((3, 3 * d)),
                  _const_spec((1, 2 * LANES)), _const_spec((1, 2 * LANES))],
        out_specs=[_row_spec(d)] * 4 + [_row_spec(2 * LANES)],
        out_shape=[jax.ShapeDtypeStruct((m, d), F32)] * 4 + [jax.ShapeDtypeStruct((m, 2 * LANES), F32)],
        compiler_params=_cparams("parallel"),
    )(h, h, h, win, wg, conv_w, ga, gb)


def _dn_scan_kernel(tbl_ref, *refs, nh, hd):
    ins, s0_ref, outs, sT_ref, s_scr = refs[:8], refs[8], refs[9:11], refs[11], refs[12]
    t = pl.program_id(0)
    n = tbl_ref[1, t]

    @pl.when(n == 0)
    def _():
        s_scr[...] = s0_ref[0]

    pk = _Packed()
    chains = []
    for e in range(2):
        q_ref, k_ref, v_ref, gd_ref = ins[4 * e:4 * e + 4]
        order = _order(1 - 2 * e)
        incl = order >= 0
        gd = gd_ref[...]
        cs = _dot_mask_exact(incl.astype(BF16), gd)
        tot = jnp.sum(gd, axis=0, keepdims=True)
        e_cs = jnp.exp(cs)
        e_end = jnp.exp(tot - cs)
        e_tot = jnp.exp(tot)
        cs_t = jnp.concatenate([cs, cs], axis=0).T
        for h in range(nh):
            sl = slice(h * hd, (h + 1) * hd)
            diff = jnp.broadcast_to(cs[:, h:h + 1], (CH, CH)) - jnp.broadcast_to(cs_t[h:h + 1, :CH], (CH, CH))
            chains.append(dict(
                e=e, sl=sl, incl=incl, strict=order > 0, beta=gd[:, nh + h:nh + h + 1],
                dec=jnp.where(incl, jnp.exp(jnp.where(incl, diff, 0.0)), 0.0),
                kh=k_ref[:, sl], qh=q_ref[:, sl], vh=v_ref[:, sl], e_cs=e_cs[:, h:h + 1],
                e_end=e_end[:, h:h + 1], e_tot=e_tot[:, h:h + 1], s_old=s_scr[e, h]))
    for c in chains:
        c["kb"] = c["kh"] * c["beta"]
        c["x"] = _dot_nt(jnp.concatenate([_bf(c["kb"]), _bf(c["qh"])], axis=0), _bf(c["kh"]))
        c["low"] = jnp.where(c["strict"], c["x"][:CH] * c["dec"], 0.0)
    packs = [chains[i:i + GRP] for i in range(0, len(chains), GRP)]
    t_inv = pk.tri_inverse([jnp.concatenate([c["low"] for c in p], axis=1) for p in packs])
    for p, tp in zip(packs, t_inv):
        for j, c in enumerate(p):
            c["t"] = _bf(tp[:, j * CH:(j + 1) * CH])
    for c in chains:
        vk = jnp.concatenate([_bf(c["vh"] * c["beta"]), _bf(c["kb"] * c["e_cs"])], axis=1)
        c["uw"] = _dot(c["t"], vk)
        c["sb"] = _bf(c["s_old"])
    for c in chains:
        c["ws"] = _dot(jnp.concatenate([_bf(c["uw"][:, hd:]), _bf(c["qh"] * c["e_cs"])], axis=0), c["sb"])
        c["v_new"] = _bf(c["uw"][:, :hd] - c["ws"][:CH])
    for c in chains:
        qk = _bf(jnp.where(c["incl"], c["x"][CH:] * c["dec"], 0.0))
        outs[c["e"]][:, c["sl"]] = c["ws"][CH:] + _dot(qk, c["v_new"])
    for c in chains:
        h = c["sl"].start // hd
        s_scr[c["e"], h] = c["s_old"] * c["e_tot"] + _dot_tn(_bf(c["kh"] * c["e_end"]), c["v_new"])

    @pl.when(n == tbl_ref[2, t] - 1)
    def _():
        sT_ref[0] = s_scr[...]


def _dn_scan(q, k, v, gd, s0, tbl):
    m, d = q.shape
    nh, hd = s0.shape[2:4]
    ins = []
    for e in range(2):
        ins += [pl.BlockSpec((CH, d), _dir_row(e, 0))] * 3 + [pl.BlockSpec((CH, LANES), _dir_row(e, e))]
    st = pl.BlockSpec((1, 2, nh, hd, hd), lambda t, tbl: (tbl[3, t], 0, 0, 0, 0))
    return pl.pallas_call(
        functools.partial(_dn_scan_kernel, nh=nh, hd=hd),
        grid_spec=pltpu.PrefetchScalarGridSpec(
            num_scalar_prefetch=1, grid=(tbl.shape[1],),
            in_specs=ins + [st],
            out_specs=[pl.BlockSpec((CH, d), _dir_row(0, 0)), pl.BlockSpec((CH, d), _dir_row(1, 0)), st],
            scratch_shapes=[pltpu.VMEM((2, nh, hd, hd), F32)]),
        out_shape=[jax.ShapeDtypeStruct((m, d), F32)] * 2 + [jax.ShapeDtypeStruct(s0.shape, F32)],
        compiler_params=_cparams("arbitrary"),
    )(tbl, *([q, k, v, gd] * 2), s0)


def _deltanet_mixer(h, state, p, dims):
    bp, bs, parts = dims
    m, d = h.shape
    dk = d // DN_HEADS
    q, k, v, zs, gd = _dn_proj(h, p["win"], p["wg"], p["conv"], p["ga"], p["gb"], bp, parts, dk)
    s0 = jnp.concatenate([jnp.zeros((bp,) + state.shape[1:], F32), state], axis=0)
    o_f, o_b, s_fin = _dn_scan(q, k, v, gd, s0, _scan_table(bp, bs, parts))
    return (o_f, o_b, zs), s_fin[:bp]


def _gate_columns(w_gate):
    dd = w_gate.shape[0]
    pad = jnp.zeros((dd, LANES - 2 * DN_HEADS), w_gate.dtype)
    cols = []
    for e in range(2):
        cols += [w_gate[:, e * DN_HEADS:(e + 1) * DN_HEADS],
                 w_gate[:, (2 + e) * DN_HEADS:(3 + e) * DN_HEADS], pad]
    return jnp.concatenate(cols, axis=1)


def _gate_lanes(x):
    pad = jnp.zeros((LANES - DN_HEADS,), x.dtype)
    return jnp.concatenate([x[0], pad, x[1], pad]).reshape(1, 2 * LANES)


def kernel(x_prompt, x_sample, state_rwkv, cache_k, cache_v, state_delta, c, c_ctx, w_mod, b_mod, ln_g, ln_b, w_fc1, w_fc2, rw_mu, rw_wrkv, rw_w0, rw_w1, rw_w2, rw_a0, rw_a1, rw_a2, rw_g1, rw_g2, rw_kk, rw_ka, rw_rk, rw_gn_g, rw_gn_b, rw_wo, at_wqkv, at_qn, at_kn, at_wo, dn_win, dn_conv, dn_alog, dn_dtb, dn_ng, dn_wo):
    bp, tp, d = x_prompt.shape
    bs, ts, _ = x_sample.shape
    assert tp == SEG and ts % SEG == 0 and bs + 1 <= SUBLANES
    parts = ts // SEG
    dims = (bp, bs, parts)
    depth = w_mod.shape[0]
    alpha = (2.0 * depth) ** 0.25

    x = jnp.concatenate([x_prompt.reshape(bp * tp, d), x_sample.reshape(bs * ts, d)], axis=0)
    cond = jnp.concatenate([c, c_ctx[None, :], jnp.zeros((SUBLANES - bs - 1, d), F32)], axis=0)
    mods = _modulation(cond, w_mod, b_mod)
    seg_row = jnp.array([bs] * bp + [b for b in range(bs) for _ in range(parts)], jnp.int32)
    modseg = [mods[l][seg_row].reshape(bp + bs * parts, N_MOD, d) for l in range(depth)]

    def mixer_in_dtype(l):
        return F32 if l % 3 == 0 else BF16

    new_rwkv, new_k, new_v, new_delta = [], [], [], []
    h = _premod(x, modseg[0], mixer_in_dtype(0))
    for l in range(depth):
        kind, j = l % 3, l // 3
        if kind == 0:
            z64 = jnp.zeros((RW_HEAD, d), F32)
            prm = dict(
                mu=rw_mu[j], wr=_bf(rw_wrkv[j, 0]), wk=_bf(rw_wrkv[j, 1]), wv=_bf(rw_wrkv[j, 2]),
                w1=_bf(jnp.concatenate([rw_w1[j, 0], rw_w1[j, 1]], axis=1)),
                w2=_bf(jnp.concatenate([jnp.concatenate([rw_w2[j, 0], z64], axis=1),
                                        jnp.concatenate([z64, rw_w2[j, 1]], axis=1)], axis=0)),
                a1=_bf(jnp.concatenate([rw_a1[j, 0], rw_a1[j, 1]], axis=1)),
                a2=_bf(jnp.concatenate([jnp.concatenate([rw_a2[j, 0], z64], axis=1),
                                        jnp.concatenate([z64, rw_a2[j, 1]], axis=1)], axis=0)),
                g1=_bf(rw_g1[j]), g2=_bf(rw_g2[j]), w0=rw_w0[j], a0=rw_a0[j],
                kk=rw_kk[j], ka=rw_ka[j], rk=rw_rk[j].reshape(-1), wo=_bf(rw_wo[j]))
            rows, s_new = _rwkv_mixer(h, state_rwkv[:, j], prm, dims)
            new_rwkv.append(s_new)
            out = dict(kern=_rw_out_ln_kernel, rows=rows, vecs=[rw_gn_g[j].reshape(1, d), rw_gn_b[j].reshape(1, d)])
        elif kind == 1:
            prm = dict(wqkv=_bf(at_wqkv[j]), qn=at_qn[j], kn=at_kn[j], wo=_bf(at_wo[j]))
            y, kp, vp = _attention_mixer(h, cache_k[:, j], cache_v[:, j], prm, dims)
            new_k.append(kp.reshape(bp, tp, -1, AT_HEAD))
            new_v.append(vp.reshape(bp, tp, -1, AT_HEAD))
            out = dict(kern=_mm_ln_kernel, rows=[y], vecs=[])
        else:
            prm = dict(win=_bf(dn_win[j, :, :4 * d]), wg=_bf(_gate_columns(dn_win[j, :, 4 * d:])),
                       conv=dn_conv[j], ga=_gate_lanes(dn_alog[j]), gb=_gate_lanes(dn_dtb[j]),
                       wo=_bf(dn_wo[j]))
            rows, s_new = _deltanet_mixer(h, state_delta[:, j], prm, dims)
            new_delta.append(s_new)
            out = dict(kern=functools.partial(_dn_out_ln_kernel, dv=d // DN_HEADS), rows=rows,
                       vecs=[jnp.tile(dn_ng[j], DN_HEADS).reshape(1, d)])
        x, h = _sublayer_ln(out["kern"], out["rows"], out["vecs"], [prm["wo"]], x, modseg[l], modseg[l],
                            ln_g[l, 0], ln_b[l, 0], alpha=alpha, gate_idx=2, sh_idx=3, h_dtype=BF16)
        nxt = min(l + 1, depth - 1)
        x, h = _sublayer_ln(_mlp_ln_kernel, [h], [], [_bf(w_fc1[l]), _bf(w_fc2[l])], x, modseg[l], modseg[nxt],
                            ln_g[l, 1], ln_b[l, 1], alpha=alpha, gate_idx=5, sh_idx=0,
                            h_dtype=mixer_in_dtype(l + 1) if l + 1 < depth else None, fc=d)
    y_prompt = x[:bp * tp].reshape(bp, tp, d)
    y_sample = x[bp * tp:].reshape(bs, ts, d)
    return (y_prompt, y_sample, jnp.stack(new_rwkv, axis=1), jnp.stack(new_k, axis=1),
            jnp.stack(new_v, axis=1), jnp.stack(new_delta, axis=1))
```

```python
import functools

import jax
import jax.numpy as jnp
import numpy as np
from jax import lax
from jax.experimental import pallas as pl
from jax.experimental.pallas import tpu as pltpu

F32 = jnp.float32
BF16 = jnp.bfloat16

SEG = 256
CH = 64
LANES = 128
SUBLANES = 8
MXU_WIDTH = 256
GRP = MXU_WIDTH // CH
VMEM_LIMIT = 48 * 1024 * 1024

N_MOD = 6
LN_EPS = 1e-5
RW_HEAD = 64
RW_GN_EPS = 64e-5
AT_HEAD = 64
AT_GROUP = 4
GRID_W = 64
ROPE_THETA = 10000.0
DN_HEADS = 8


def _cparams(*sem):
    return pltpu.CompilerParams(dimension_semantics=sem, vmem_limit_bytes=VMEM_LIMIT)


def _dot(a, b):
    return jnp.dot(a, b, preferred_element_type=F32)


def _dot_nt(a, b):
    return lax.dot_general(a, b, (((1,), (1,)), ((), ())), preferred_element_type=F32)


def _dot_tn(a, b):
    return lax.dot_general(a, b, (((0,), (0,)), ((), ())), preferred_element_type=F32)


def _bf(x):
    return x.astype(BF16)


def _split3(x):
    hi = _bf(x)
    r1 = x - hi.astype(F32)
    mid = _bf(r1)
    lo = _bf(r1 - mid.astype(F32))
    return hi, mid, lo


def _dot_mask_exact(mask_bf, x):
    hi, mid, lo = _split3(x)
    return _dot(mask_bf, hi) + _dot(mask_bf, mid) + _dot(mask_bf, lo)


def _segsum(x, seg):
    w = x.shape[1]
    li = lax.broadcasted_iota(jnp.int32, (LANES, LANES), 0) // seg
    lj = lax.broadcasted_iota(jnp.int32, (LANES, LANES), 1) // seg
    bd = (li == lj).astype(BF16)
    outs = []
    for j in range(w // LANES):
        xs = x[:, LANES * j:LANES * (j + 1)]
        hi = _bf(xs)
        lo = _bf(xs - hi.astype(F32))
        outs.append(_dot(hi, bd) + _dot(lo, bd))
    return outs[0] if len(outs) == 1 else jnp.concatenate(outs, axis=1)


def _softplus(x):
    return jnp.maximum(x, 0.0) + jnp.log1p(jnp.exp(-jnp.abs(x)))


def _sigmoid(x):
    return 1.0 / (1.0 + jnp.exp(-x))


def _seg_flags(i, bp, parts):
    j = (i - bp) % parts
    is_ctx = i < bp
    return jnp.logical_or(is_ctx, j == 0), jnp.logical_or(is_ctx, j == parts - 1)


def _shift_prev(x, xp8, is_start):
    rolled = pltpu.roll(x, 1, 0)
    row0 = jnp.where(is_start, 0.0, xp8[SUBLANES - 1:SUBLANES, :])
    rid = lax.broadcasted_iota(jnp.int32, x.shape, 0)
    return jnp.where(rid == 0, row0, rolled)


def _shift_next(x, xn8, is_end):
    n = x.shape[0]
    rolled = pltpu.roll(x, n - 1, 0)
    rowl = jnp.where(is_end, 0.0, xn8[0:1, :])
    rid = lax.broadcasted_iota(jnp.int32, x.shape, 0)
    return jnp.where(rid == n - 1, rowl, rolled)


def _row_spec(w):
    return pl.BlockSpec((SEG, w), lambda i: (i, 0))


def _halo_specs(w, m_rows, rows=SUBLANES):
    per = SEG // rows
    last = m_rows // rows - 1
    return (pl.BlockSpec((rows, w), lambda i: (jnp.maximum(i * per - 1, 0), 0)),
            pl.BlockSpec((rows, w), lambda i: (jnp.minimum((i + 1) * per, last), 0)))


def _const_spec(shape):
    nd = len(shape)
    return pl.BlockSpec(shape, lambda i: (0,) * nd)


def _segvec_spec(k, w):
    return pl.BlockSpec((1, k, w), lambda i: (i, 0, 0))


def _mod_kernel(c_ref, w_ref, b_ref, o_ref):
    c = c_ref[...]
    s = c * _sigmoid(c)
    o_ref[0] = _dot(_bf(s), _bf(w_ref[0])) + b_ref[0]


def _modulation(cond, w_mod, b_mod, tn=1536):
    depth, d, n = w_mod.shape
    return pl.pallas_call(
        _mod_kernel,
        grid=(depth, n // tn),
        in_specs=[pl.BlockSpec((SUBLANES, d), lambda l, j: (0, 0)),
                  pl.BlockSpec((1, d, tn), lambda l, j: (l, 0, j)),
                  pl.BlockSpec((1, 1, tn), lambda l, j: (l, 0, j))],
        out_specs=pl.BlockSpec((1, SUBLANES, tn), lambda l, j: (l, 0, j)),
        out_shape=jax.ShapeDtypeStruct((depth, SUBLANES, n), F32),
        compiler_params=_cparams("parallel", "parallel"),
    )(cond, w_mod, b_mod.reshape(depth, 1, n))


def _premod_kernel(x_ref, m_ref, h_ref):
    h = x_ref[...] * (1.0 + m_ref[0, 1:2, :]) + m_ref[0, 0:1, :]
    h_ref[...] = h.astype(h_ref.dtype)


def _premod(x, modseg, out_dtype):
    m, d = x.shape
    return pl.pallas_call(
        _premod_kernel,
        grid=(m // SEG,),
        in_specs=[_row_spec(d), _segvec_spec(N_MOD, d)],
        out_specs=_row_spec(d),
        out_shape=jax.ShapeDtypeStruct((m, d), out_dtype),
        compiler_params=_cparams("parallel"),
    )(x, modseg)


def _seg_rows(m_ref, idx, spt):
    d = m_ref.shape[2]
    rows = [jnp.broadcast_to(m_ref[s, idx:idx + 1, :], (SEG, d)) for s in range(spt)]
    return rows[0] if spt == 1 else jnp.concatenate(rows, axis=0)


def _ln_epilogue(x, delta, m_ref, mn_ref, g_ref, b_ref, xo_ref, ho_ref, *, alpha, gate_idx, sh_idx, spt):
    y = alpha * x + _seg_rows(m_ref, gate_idx, spt) * delta
    mu = jnp.mean(y, axis=-1, keepdims=True)
    yc = y - mu
    var = jnp.mean(yc * yc, axis=-1, keepdims=True)
    xn = yc * lax.rsqrt(var + LN_EPS) * g_ref[...] + b_ref[...]
    xo_ref[...] = xn
    if ho_ref is not None:
        h = xn * (1.0 + _seg_rows(mn_ref, sh_idx + 1, spt)) + _seg_rows(mn_ref, sh_idx, spt)
        ho_ref[...] = h.astype(ho_ref.dtype)


def _mm_ln_kernel(y_ref, w_ref, x_ref, m_ref, mn_ref, g_ref, b_ref, xo_ref, *rest, **kw):
    delta = _dot(y_ref[...], w_ref[...])
    _ln_epilogue(x_ref[...], delta, m_ref, mn_ref, g_ref, b_ref, xo_ref, rest[0] if rest else None, **kw)


def _rw_out_ln_kernel(of_ref, ob_ref, bonus_ref, gate_ref, gng_ref, gnb_ref, w_ref,
                      x_ref, m_ref, mn_ref, g_ref, b_ref, xo_ref, *rest, **kw):
    o = of_ref[...] + ob_ref[...]
    oc = o - _segsum(o, RW_HEAD) * (1.0 / RW_HEAD)
    var = _segsum(oc * oc, RW_HEAD) * (1.0 / RW_HEAD)
    y = oc * lax.rsqrt(var + RW_GN_EPS) * gng_ref[...] + gnb_ref[...]
    delta = _dot(_bf((y + bonus_ref[...]) * gate_ref[...]), w_ref[...])
    _ln_epilogue(x_ref[...], delta, m_ref, mn_ref, g_ref, b_ref, xo_ref, rest[0] if rest else None, **kw)


def _dn_out_ln_kernel(of_ref, ob_ref, zs_ref, ng_ref, w_ref,
                      x_ref, m_ref, mn_ref, g_ref, b_ref, xo_ref, *rest, dv, **kw):
    o = of_ref[...] + ob_ref[...]
    y = o * lax.rsqrt(_segsum(o * o, dv) * (1.0 / dv) + 1e-6) * ng_ref[...]
    delta = _dot(_bf(y * zs_ref[...]), w_ref[...])
    _ln_epilogue(x_ref[...], delta, m_ref, mn_ref, g_ref, b_ref, xo_ref, rest[0] if rest else None, **kw)


def _mlp_ln_kernel(h_ref, w1_ref, w2_ref, x_ref, m_ref, mn_ref, g_ref, b_ref, xo_ref, *rest, fc, **kw):
    hb = h_ref[...]
    delta = jnp.zeros(x_ref.shape, F32)
    for j in range(w1_ref.shape[1] // fc):
        u = jnp.square(jnp.maximum(_dot(hb, w1_ref[:, j * fc:(j + 1) * fc]), 0.0))
        delta = delta + _dot(_bf(u), w2_ref[j * fc:(j + 1) * fc, :])
    _ln_epilogue(x_ref[...], delta, m_ref, mn_ref, g_ref, b_ref, xo_ref, rest[0] if rest else None, **kw)


def _resident(shape):
    nd = len(shape)
    return pl.BlockSpec(shape, lambda i: (0,) * nd, pipeline_mode=pl.Buffered(1))


def _sublayer_ln(kern, rows, vecs, weights, x, modseg, modseg_next, g, b, *, alpha, gate_idx, sh_idx,
                 h_dtype, **static):
    m, d = x.shape
    nseg = modseg.shape[0]
    spt = 2 if nseg % 2 == 0 else 1
    tm = spt * SEG
    row = lambda w: pl.BlockSpec((tm, w), lambda i: (i, 0))
    segv = pl.BlockSpec((spt, N_MOD, d), lambda i: (i, 0, 0))
    out_shape = [jax.ShapeDtypeStruct((m, d), F32)]
    out_specs = [row(d)]
    if h_dtype is not None:
        out_shape.append(jax.ShapeDtypeStruct((m, d), h_dtype))
        out_specs.append(row(d))
    res = pl.pallas_call(
        functools.partial(kern, alpha=alpha, gate_idx=gate_idx, sh_idx=sh_idx, spt=spt, **static),
        grid=(m // tm,),
        in_specs=[row(r.shape[1]) for r in rows] + [_const_spec(v.shape) for v in vecs]
        + [_resident(w.shape) for w in weights]
        + [row(d), segv, segv, _const_spec((1, d)), _const_spec((1, d))],
        out_specs=out_specs,
        out_shape=out_shape,
        compiler_params=_cparams("parallel"),
    )(*rows, *vecs, *weights, x, modseg, modseg_next, g.reshape(1, d), b.reshape(1, d))
    return (res[0], res[1]) if h_dtype is not None else (res[0], None)


class _Packed:
    def __init__(self):
        gw = GRP * CH
        self.ti = lax.broadcasted_iota(jnp.int32, (CH, gw), 0)
        lane = lax.broadcasted_iota(jnp.int32, (CH, gw), 1)
        self.si = lane % CH
        self.grp = lane // CH
        self.eye = (self.ti == self.si).astype(F32)
        self.merge = []
        b = 1
        while b < CH:
            self.merge.append(jnp.logical_and(self.ti // (2 * b) == self.si // (2 * b),
                                              self.ti // b != self.si // b))
            b *= 2

    def bd(self, x):
        return jnp.concatenate([_bf(jnp.where(self.grp == j, x, 0.0)) for j in range(GRP)], axis=0)

    def tri_inverse(self, lows):
        ts = [self.eye - jnp.where(self.merge[0], low, 0.0) for low in lows]
        for mk in self.merge[1:]:
            xs = [_dot(_bf(jnp.where(mk, low, 0.0)), self.bd(t)) for low, t in zip(lows, ts)]
            ts = [t - _dot(_bf(t), self.bd(x)) for t, x in zip(ts, xs)]
        return ts


def _order(sign):
    return (lax.broadcasted_iota(jnp.int32, (CH, CH), 0) - lax.broadcasted_iota(jnp.int32, (CH, CH), 1)) * sign


def _scan_table(bp, bs, parts):
    ncp = SEG // CH
    ncs = parts * ncp
    steps = [(b * ncp, n, ncp, b) for b in range(bp) for n in range(ncp)]
    steps += [(bp * ncp + b * ncs, n, ncs, bp + b) for b in range(bs) for n in range(ncs)]
    return jnp.asarray(np.array(steps, np.int32).T)


def _dir_row(e, col):
    def index(t, tbl):
        n = tbl[1, t]
        return tbl[0, t] + n + e * (tbl[2, t] - 1 - 2 * n), col
    return index


def _rw_proj_kernel(h_ref, hp_ref, hn_ref, mu_ref, wr_ref, wk_ref, wv_ref, w1_ref, w2_ref, a1_ref, a2_ref,
                    g1_ref, g2_ref, w0_ref, a0_ref, kk_ref, ka_ref, rk_ref,
                    r_ref, v_ref, g_ref, ld_ref, be_ref, kd_ref, alpha_ref, bonus_ref, *, bp, parts):
    d = h_ref.shape[1]
    is_start, is_end = _seg_flags(pl.program_id(0), bp, parts)
    h = h_ref[...]
    xx = 0.5 * (_shift_prev(h, hp_ref[...], is_start) + _shift_next(h, hn_ref[...], is_end)) - h

    def mix(j):
        return _bf(h + xx * mu_ref[j:j + 1, :])

    r = _dot(mix(0), wr_ref[...])
    k = _dot(mix(2), wk_ref[...])
    v = _dot(mix(3), wv_ref[...])
    wl = _dot(_bf(jnp.tanh(_dot(mix(1), w1_ref[...]))), w2_ref[...])
    al = _dot(_bf(_dot(mix(4), a1_ref[...])), a2_ref[...])
    g_ref[...] = _bf(_dot(_bf(_sigmoid(_dot(mix(5), g1_ref[...]))), g2_ref[...]))
    r_ref[...] = r
    v_ref[...] = v
    kkn = k * kk_ref[...]
    kk = kkn * lax.rsqrt(_segsum(kkn * kkn, RW_HEAD) + 1e-6)
    alpha_ref[...] = -kk
    bon = jnp.zeros_like(r)
    for e in range(2):
        cols = slice(e * d, (e + 1) * d)
        ld_ref[:, cols] = -float(np.exp(-0.5)) * _sigmoid(w0_ref[e:e + 1, :] + wl[:, cols])
        a = _sigmoid(a0_ref[e:e + 1, :] + al[:, cols])
        be_ref[:, cols] = kk * a
        kd = k * (1.0 + (a - 1.0) * ka_ref[...])
        kd_ref[:, cols] = kd
        bon = bon + r * kd * rk_ref[...]
    bonus_ref[...] = _bf(_segsum(bon, RW_HEAD) * v)


def _rw_proj(h, p, bp, parts):
    m, d = h.shape
    hp, hn = _halo_specs(d, m)
    weights = [p[n] for n in ("wr", "wk", "wv", "w1", "w2", "a1", "a2", "g1", "g2")]
    vecs = [p["w0"], p["a0"], p["kk"].reshape(1, d), p["ka"].reshape(1, d), p["rk"].reshape(1, d)]
    return pl.pallas_call(
        functools.partial(_rw_proj_kernel, bp=bp, parts=parts),
        grid=(m // SEG,),
        in_specs=[_row_spec(d), hp, hn, _const_spec((6, d))] + [_resident(w.shape) for w in weights]
        + [_const_spec(x.shape) for x in vecs],
        out_specs=[_row_spec(d)] * 3 + [_row_spec(2 * d)] * 3 + [_row_spec(d)] * 2,
        out_shape=[jax.ShapeDtypeStruct((m, d), F32)] * 2 + [jax.ShapeDtypeStruct((m, d), BF16)]
        + [jax.ShapeDtypeStruct((m, 2 * d), F32)] * 3
        + [jax.ShapeDtypeStruct((m, d), F32), jax.ShapeDtypeStruct((m, d), BF16)],
        compiler_params=_cparams("parallel"),
    )(h, h, h, p["mu"], *weights, *vecs)


def _rw_scan_kernel(tbl_ref, *refs, ng, hd, bp):
    ins, s0_ref, outs, sT_ref, s_scr = refs[:12], refs[12], refs[13:15], refs[15], refs[16]
    t = pl.program_id(0)
    n = tbl_ref[1, t]
    assert hd == CH
    gw = GRP * hd

    @pl.when(n == 0)
    def _():
        for e in range(2):
            for g in range(ng):
                rows = []
                for j in range(GRP):
                    parts_ = [jnp.zeros((hd, j * hd), F32)] if j else []
                    parts_.append(jnp.where(tbl_ref[3, t] < bp, 0.0, s0_ref[0, e, g * GRP + j]))
                    if j < GRP - 1:
                        parts_.append(jnp.zeros((hd, (GRP - 1 - j) * hd), F32))
                    rows.append(jnp.concatenate(parts_, axis=1))
                s_scr[e, g] = jnp.concatenate(rows, axis=0)

    pk = _Packed()
    bd = pk.bd
    own = (lax.broadcasted_iota(jnp.int32, (gw, gw), 0) // hd
           == lax.broadcasted_iota(jnp.int32, (gw, gw), 1) // hd)
    cols = lambda g: slice(g * gw, (g + 1) * gw)
    chains = []
    for e in range(2):
        r_ref, ld_ref, al_ref, be_ref, kd_ref, v_ref = ins[6 * e:6 * e + 6]
        sign = 1 - 2 * e
        ahead = (pk.ti - pk.si) * sign
        ld = ld_ref[...]
        cs = _dot_mask_exact((_order(sign) >= 0).astype(BF16), ld)
        tot = jnp.sum(ld, axis=0, keepdims=True)
        g_end = jnp.exp(tot - cs)
        g_inv = jnp.exp(-cs)
        be = be_ref[...]
        kd = kd_ref[...]
        full = dict(ab=al_ref[...] * jnp.exp(cs - ld), qb=r_ref[...] * jnp.exp(cs), bt=be * g_inv,
                    kt=kd * g_inv, b_end=be * g_end, k_end=kd * g_end, v=v_ref[...], g_tot=jnp.exp(tot))
        for g in range(ng):
            c = {k: x[:, cols(g)] for k, x in full.items()}
            c.update(e=e, g=g, incl=ahead >= 0, strict=ahead > 0, s_old=s_scr[e, g])
            chains.append(c)
    for c in chains:
        c["sb"] = _bf(c["s_old"])
        c["abb"] = _bf(c["ab"])
        c["qbb"] = _bf(c["qb"])
        c["vbd"] = bd(c["v"])
        aq = jnp.concatenate([c["abb"], c["qbb"]], axis=0)
        c["xb"] = _dot_nt(aq, bd(c["bt"]))
        c["xk"] = _dot_nt(aq, bd(c["kt"]))
        c["sq"] = _dot_nt(aq, c["sb"])
    for c in chains:
        a_k = jnp.concatenate([jnp.where(c["strict"], c["xk"][:CH], 0.0),
                               jnp.where(c["incl"], c["xk"][CH:], 0.0)], axis=0)
        c["av"] = _dot(_bf(a_k), c["vbd"])
    t_inv = pk.tri_inverse([jnp.where(c["strict"], -c["xb"][:CH], 0.0) for c in chains])
    for c, tc in zip(chains, t_inv):
        c["u"] = _dot(_bf(tc), bd(c["sq"][:CH] + c["av"][:CH]))
    for c in chains:
        a_qb = _bf(jnp.where(c["incl"], c["xb"][CH:], 0.0))
        outs[c["e"]][:, cols(c["g"])] = c["sq"][CH:] + c["av"][CH:] + _dot(a_qb, bd(c["u"]))
    for c in chains:
        uv = jnp.concatenate([_bf(c["u"]), _bf(c["v"])], axis=0)
        bk = jnp.concatenate([_bf(c["b_end"]), _bf(c["k_end"])], axis=0)
        s_scr[c["e"], c["g"]] = c["s_old"] * c["g_tot"] + jnp.where(own, _dot_tn(uv, bk), 0.0)

    @pl.when(n == tbl_ref[2, t] - 1)
    def _():
        for e in range(2):
            for g in range(ng):
                for j in range(GRP):
                    sT_ref[0, e, g * GRP + j] = s_scr[e, g, j * hd:(j + 1) * hd, j * hd:(j + 1) * hd]


def _state_specs(s_lat, bp):
    blk = (1,) + s_lat.shape[1:]
    return (pl.BlockSpec(blk, lambda t, tbl: (jnp.maximum(tbl[3, t] - bp, 0), 0, 0, 0, 0)),
            pl.BlockSpec(blk, lambda t, tbl: (tbl[3, t], 0, 0, 0, 0)),
            jax.ShapeDtypeStruct((bp + s_lat.shape[0],) + s_lat.shape[1:], F32))


def _rw_scan(r, ld, alpha, be, kd, v, s_lat, tbl, bp):
    m, d = r.shape
    nh, hd = s_lat.shape[2:4]
    ng, gw = nh // GRP, GRP * hd
    ins = []
    for e in range(2):
        ins += [pl.BlockSpec((CH, d), _dir_row(e, c)) for c in (0, e, 0, e, e, 0)]
    st_in, st_out, st_shape = _state_specs(s_lat, bp)
    return pl.pallas_call(
        functools.partial(_rw_scan_kernel, ng=ng, hd=hd, bp=bp),
        grid_spec=pltpu.PrefetchScalarGridSpec(
            num_scalar_prefetch=1, grid=(tbl.shape[1],),
            in_specs=ins + [st_in],
            out_specs=[pl.BlockSpec((CH, d), _dir_row(0, 0)), pl.BlockSpec((CH, d), _dir_row(1, 0)), st_out],
            scratch_shapes=[pltpu.VMEM((2, ng, gw, gw), F32)]),
        out_shape=[jax.ShapeDtypeStruct((m, d), F32)] * 2 + [st_shape],
        compiler_params=_cparams("arbitrary"),
    )(tbl, *([r, ld, alpha, be, kd, v] * 2), s_lat)


def _rwkv_mixer(h, state, p, dims):
    bp, bs, parts = dims
    m, d = h.shape
    r, v, g, ld, be, kd, alpha, bonus = _rw_proj(h, p, bp, parts)
    o_f, o_b, s_fin = _rw_scan(r, ld, alpha, be, kd, v, state, _scan_table(bp, bs, parts), bp)
    return (o_f, o_b, bonus, g), s_fin[:bp]


def _rope(y, cos, sin):
    w = y.shape[1]
    up = pltpu.roll(y, w - AT_HEAD // 4, 1)
    dn = pltpu.roll(y, AT_HEAD // 4, 1)
    lane = lax.broadcasted_iota(jnp.int32, y.shape, 1)
    first = (lane % (AT_HEAD // 2)) < (AT_HEAD // 4)
    return y * cos + jnp.where(first, up, dn) * sin


def _at_proj_kernel(h_ref, w_ref, cos_ref, sin_ref, qg_ref, kg_ref, q_ref, kr_ref, kn_ref, v_ref, vb_ref,
                    *, dq, dk):
    q = _dot(h_ref[...], w_ref[:, :dq])
    k = _dot(h_ref[...], w_ref[:, dq:dq + dk])
    v = _dot(h_ref[...], w_ref[:, dq + dk:])
    qn = q * lax.rsqrt(_segsum(q * q, AT_HEAD) * (1.0 / AT_HEAD) + 1e-6) * qg_ref[...]
    kn = k * lax.rsqrt(_segsum(k * k, AT_HEAD) * (1.0 / AT_HEAD) + 1e-6) * kg_ref[...]
    kn_ref[...] = kn
    v_ref[...] = v
    vb_ref[...] = _bf(v)
    q_ref[...] = _bf(_rope(qn, cos_ref[...], sin_ref[...]))
    kr_ref[...] = _bf(_rope(kn, cos_ref[:, :dk], sin_ref[:, :dk]))


def _at_proj(h, wqkv, cos_t, sin_t, qg, kg, bp, parts, dq, dk):
    m, d = h.shape
    tab = pl.BlockSpec((SEG, dq), lambda i: (jnp.where(i < bp, 0, 1 + (i - bp) % parts), 0))
    return pl.pallas_call(
        functools.partial(_at_proj_kernel, dq=dq, dk=dk),
        grid=(m // SEG,),
        in_specs=[_row_spec(d), _resident(wqkv.shape), tab, tab, _const_spec((1, dq)), _const_spec((1, dk))],
        out_specs=[_row_spec(dq)] + [_row_spec(dk)] * 4,
        out_shape=[jax.ShapeDtypeStruct((m, dq), BF16), jax.ShapeDtypeStruct((m, dk), BF16),
                   jax.ShapeDtypeStruct((m, dk), F32), jax.ShapeDtypeStruct((m, dk), F32),
                   jax.ShapeDtypeStruct((m, dk), BF16)],
        compiler_params=_cparams("parallel"),
    )(h, wqkv, cos_t, sin_t, qg, kg)


def _attn_kernel(q_ref, *refs, scale):
    kv_refs, o_ref = refs[:-1], refs[-1]
    k_refs, v_refs = kv_refs[0::2], kv_refs[1::2]
    nkv = k_refs[0].shape[2] // AT_HEAD
    for g in range(nkv):
        cols = slice(g * AT_HEAD, (g + 1) * AT_HEAD)
        kg = [_bf(k_ref[0, :, cols]) for k_ref in k_refs]
        vg = [_bf(v_ref[0, :, cols]) for v_ref in v_refs]
        sls = [slice((g * AT_GROUP + j) * AT_HEAD, (g * AT_GROUP + j + 1) * AT_HEAD) for j in range(AT_GROUP)]
        s = [[_dot_nt(q_ref[:, sl], kp) * scale for kp in kg] for sl in sls]
        s = [x[0] if len(x) == 1 else jnp.concatenate(x, axis=1) for x in s]
        p = [jnp.exp(x - jnp.max(x, axis=-1, keepdims=True)) for x in s]
        l = [jnp.sum(x, axis=-1, keepdims=True) for x in p]
        o = []
        for x in p:
            acc, off = None, 0
            for vp in vg:
                part = _dot(_bf(x[:, off:off + vp.shape[0]]), vp)
                acc = part if acc is None else acc + part
                off += vp.shape[0]
            o.append(acc)
        for sl, ox, lx in zip(sls, o, l):
            o_ref[:, sl] = (ox / lx).astype(o_ref.dtype)


def _attention(q, kv_parts, *, base, nq):
    dq = q.shape[1]
    b = kv_parts[0][0].shape[0]
    flat = [a for kv in kv_parts for a in kv]
    return pl.pallas_call(
        functools.partial(_attn_kernel, scale=AT_HEAD ** -0.5),
        grid=(b, nq),
        in_specs=[pl.BlockSpec((SEG, dq), lambda bb, j: (base + bb * nq + j, 0))]
        + [pl.BlockSpec((1,) + a.shape[1:], lambda bb, j: (bb, 0, 0)) for a in flat],
        out_specs=pl.BlockSpec((SEG, dq), lambda bb, j: (bb * nq + j, 0)),
        out_shape=jax.ShapeDtypeStruct((b * nq * SEG, dq), BF16),
        compiler_params=_cparams("parallel", "parallel"),
    )(q, *flat)


def _rope_tables(t, heads):
    rows = t // GRID_W
    row = jnp.repeat(jnp.arange(rows), GRID_W)
    col = jnp.tile(jnp.arange(GRID_W), rows)
    n_freq = AT_HEAD // 4
    inv_freq = ROPE_THETA ** (-jnp.arange(n_freq, dtype=F32) / n_freq)
    ang = jnp.stack([row, col], axis=-1).astype(F32)[:, :, None] * inv_freq
    cos = jnp.repeat(jnp.cos(ang)[:, :, None, :], 2, axis=2).reshape(t, AT_HEAD)
    sin = jnp.sin(ang)
    sin = jnp.stack([-sin, sin], axis=2).reshape(t, AT_HEAD)
    ident = (jnp.ones((SEG, AT_HEAD), F32), jnp.zeros((SEG, AT_HEAD), F32))
    cos = jnp.tile(jnp.concatenate([ident[0], cos], axis=0), (1, heads))
    sin = jnp.tile(jnp.concatenate([ident[1], sin], axis=0), (1, heads))
    return cos, sin


def _attention_mixer(h, cache_k, cache_v, p, dims):
    bp, bs, parts = dims
    m, d = h.shape
    dk = cache_k.shape[-1] * cache_k.shape[-2]
    heads = d // AT_HEAD
    cos_t, sin_t = _rope_tables(parts * SEG, heads)
    q, kr, kn, v, vb = _at_proj(h, p["wqkv"], cos_t, sin_t, jnp.tile(p["qn"], heads).reshape(1, d),
                                jnp.tile(p["kn"], dk // AT_HEAD).reshape(1, dk), bp, parts, d, dk)
    np_ = bp * SEG
    o_p = _attention(q, [(kr[:np_].reshape(bp, SEG, dk), vb[:np_].reshape(bp, SEG, dk))], base=0, nq=1)
    ctx = (cache_k.reshape(bs, -1, dk), cache_v.reshape(bs, -1, dk))
    new = (kr[np_:].reshape(bs, parts * SEG, dk), vb[np_:].reshape(bs, parts * SEG, dk))
    o_s = _attention(q, [ctx, new], base=bp, nq=parts)
    return jnp.concatenate([o_p, o_s], axis=0), kn[:np_], v[:np_]


HALO16 = 16


def _dn_proj_kernel(h_ref, hp_ref, hn_ref, win_ref, wg_ref, cw_ref, ga_ref, gb_ref,
                    q_ref, k_ref, v_ref, zs_ref, gd_ref, *, bp, parts, d, dk):
    is_start, is_end = _seg_flags(pl.program_id(0), bp, parts)
    h = h_ref[...]
    hcat = jnp.concatenate([hp_ref[...], h, hn_ref[...]], axis=0)
    n = hcat.shape[0]
    rid = lax.broadcasted_iota(jnp.int32, (n, d), 0)
    edge_prev = jnp.logical_and(is_start, rid == HALO16)
    edge_next = jnp.logical_and(is_end, rid == HALO16 + SEG - 1)
    outs = []
    for j in range(3):
        cols = slice(j * d, (j + 1) * d)
        x = _dot(hcat, win_ref[:, cols])
        prev = jnp.where(edge_prev, 0.0, pltpu.roll(x, 1, 0))
        nxt = jnp.where(edge_next, 0.0, pltpu.roll(x, n - 1, 0))
        c = cw_ref[0:1, cols] * prev + cw_ref[1:2, cols] * x + cw_ref[2:3, cols] * nxt
        c = c[HALO16:HALO16 + SEG]
        outs.append(c * _sigmoid(c))
    q, k, v = outs
    q_ref[...] = q * lax.rsqrt(_segsum(q * q, dk) + 1e-6) * (dk ** -0.5)
    k_ref[...] = k * lax.rsqrt(_segsum(k * k, dk) + 1e-6)
    v_ref[...] = v
    z = _dot(h, win_ref[:, 3 * d:])
    zs_ref[...] = _bf(z * _sigmoid(z))
    gt = _dot(h, wg_ref[...])
    lane = lax.broadcasted_iota(jnp.int32, gt.shape, 1) % LANES
    gd_ref[...] = jnp.where(lane < DN_HEADS, -jnp.exp(ga_ref[...]) * _softplus(gt + gb_ref[...]), _sigmoid(gt))


def _dn_proj(h, win, wg, conv_w, ga, gb, bp, parts, dk):
    m, d = h.shape
    hp, hn = _halo_specs(d, m, HALO16)
    return pl.pallas_call(
        functools.partial(_dn_proj_kernel, bp=bp, parts=parts, d=d, dk=dk),
        grid=(m // SEG,),
        in_specs=[_row_spec(d), hp, hn, _resident(win.shape), _resident(wg.shape), _const_spec((3, 3 * d)),
                  _const_spec((1, 2 * LANES)), _const_spec((1, 2 * LANES))],
        out_specs=[_row_spec(d)] * 4 + [_row_spec(2 * LANES)],
        out_shape=[jax.ShapeDtypeStruct((m, d), F32)] * 3 + [jax.ShapeDtypeStruct((m, d), BF16),
                                                           jax.ShapeDtypeStruct((m, 2 * LANES), F32)],
        compiler_params=_cparams("parallel"),
    )(h, h, h, win, wg, conv_w, ga, gb)


def _dn_scan_kernel(tbl_ref, *refs, nh, hd, bp):
    ins, s0_ref, outs, sT_ref, s_scr = refs[:8], refs[8], refs[9:11], refs[11], refs[12]
    t = pl.program_id(0)
    n = tbl_ref[1, t]

    @pl.when(n == 0)
    def _():
        s_scr[...] = jnp.where(tbl_ref[3, t] < bp, 0.0, s0_ref[0])

    pk = _Packed()
    chains = []
    for e in range(2):
        q_ref, k_ref, v_ref, gd_ref = ins[4 * e:4 * e + 4]
        order = _order(1 - 2 * e)
        incl = order >= 0
        gd = gd_ref[...]
        cs = _dot_mask_exact(incl.astype(BF16), gd)
        tot = jnp.sum(gd, axis=0, keepdims=True)
        e_cs = jnp.exp(cs)
        e_end = jnp.exp(tot - cs)
        e_tot = jnp.exp(tot)
        cs_t = jnp.concatenate([cs, cs], axis=0).T
        for h in range(nh):
            sl = slice(h * hd, (h + 1) * hd)
            diff = jnp.broadcast_to(cs[:, h:h + 1], (CH, CH)) - jnp.broadcast_to(cs_t[h:h + 1, :CH], (CH, CH))
            chains.append(dict(
                e=e, sl=sl, incl=incl, strict=order > 0, beta=gd[:, nh + h:nh + h + 1],
                dec=jnp.where(incl, jnp.exp(jnp.where(incl, diff, 0.0)), 0.0),
                kh=k_ref[:, sl], qh=q_ref[:, sl], vh=v_ref[:, sl], e_cs=e_cs[:, h:h + 1],
                e_end=e_end[:, h:h + 1], e_tot=e_tot[:, h:h + 1], s_old=s_scr[e, h]))
    for c in chains:
        c["kb"] = c["kh"] * c["beta"]
        c["x"] = _dot_nt(jnp.concatenate([_bf(c["kb"]), _bf(c["qh"])], axis=0), _bf(c["kh"]))
        c["low"] = jnp.where(c["strict"], c["x"][:CH] * c["dec"], 0.0)
    packs = [chains[i:i + GRP] for i in range(0, len(chains), GRP)]
    t_inv = pk.tri_inverse([jnp.concatenate([c["low"] for c in p], axis=1) for p in packs])
    for p, tp in zip(packs, t_inv):
        for j, c in enumerate(p):
            c["t"] = _bf(tp[:, j * CH:(j + 1) * CH])
    for c in chains:
        vk = jnp.concatenate([_bf(c["vh"] * c["beta"]), _bf(c["kb"] * c["e_cs"])], axis=1)
        c["uw"] = _dot(c["t"], vk)
        c["sb"] = _bf(c["s_old"])
    for c in chains:
        c["ws"] = _dot(jnp.concatenate([_bf(c["uw"][:, hd:]), _bf(c["qh"] * c["e_cs"])], axis=0), c["sb"])
        c["v_new"] = _bf(c["uw"][:, :hd] - c["ws"][:CH])
    for c in chains:
        qk = _bf(jnp.where(c["incl"], c["x"][CH:] * c["dec"], 0.0))
        outs[c["e"]][:, c["sl"]] = c["ws"][CH:] + _dot(qk, c["v_new"])
    for c in chains:
        h = c["sl"].start // hd
        s_scr[c["e"], h] = c["s_old"] * c["e_tot"] + _dot_tn(_bf(c["kh"] * c["e_end"]), c["v_new"])

    @pl.when(n == tbl_ref[2, t] - 1)
    def _():
        sT_ref[0] = s_scr[...]


def _dn_scan(q, k, v, gd, s_lat, tbl, bp):
    m, d = q.shape
    nh, hd = s_lat.shape[2:4]
    ins = []
    for e in range(2):
        ins += [pl.BlockSpec((CH, d), _dir_row(e, 0))] * 3 + [pl.BlockSpec((CH, LANES), _dir_row(e, e))]
    st_in, st_out, st_shape = _state_specs(s_lat, bp)
    return pl.pallas_call(
        functools.partial(_dn_scan_kernel, nh=nh, hd=hd, bp=bp),
        grid_spec=pltpu.PrefetchScalarGridSpec(
            num_scalar_prefetch=1, grid=(tbl.shape[1],),
            in_specs=ins + [st_in],
            out_specs=[pl.BlockSpec((CH, d), _dir_row(0, 0)), pl.BlockSpec((CH, d), _dir_row(1, 0)), st_out],
            scratch_shapes=[pltpu.VMEM((2, nh, hd, hd), F32)]),
        out_shape=[jax.ShapeDtypeStruct((m, d), F32)] * 2 + [st_shape],
        compiler_params=_cparams("arbitrary"),
    )(tbl, *([q, k, v, gd] * 2), s_lat)


def _deltanet_mixer(h, state, p, dims):
    bp, bs, parts = dims
    m, d = h.shape
    dk = d // DN_HEADS
    q, k, v, zs, gd = _dn_proj(h, p["win"], p["wg"], p["conv"], p["ga"], p["gb"], bp, parts, dk)
    o_f, o_b, s_fin = _dn_scan(q, k, v, gd, state, _scan_table(bp, bs, parts), bp)
    return (o_f, o_b, zs), s_fin[:bp]


def _gate_columns(w_gate):
    dd = w_gate.shape[0]
    pad = jnp.zeros((dd, LANES - 2 * DN_HEADS), w_gate.dtype)
    cols = []
    for e in range(2):
        cols += [w_gate[:, e * DN_HEADS:(e + 1) * DN_HEADS],
                 w_gate[:, (2 + e) * DN_HEADS:(3 + e) * DN_HEADS], pad]
    return jnp.concatenate(cols, axis=1)


def _gate_lanes(x):
    pad = jnp.zeros((LANES - DN_HEADS,), x.dtype)
    return jnp.concatenate([x[0], pad, x[1], pad]).reshape(1, 2 * LANES)


def kernel(x_prompt, x_sample, state_rwkv, cache_k, cache_v, state_delta, c, c_ctx, w_mod, b_mod, ln_g, ln_b, w_fc1, w_fc2, rw_mu, rw_wrkv, rw_w0, rw_w1, rw_w2, rw_a0, rw_a1, rw_a2, rw_g1, rw_g2, rw_kk, rw_ka, rw_rk, rw_gn_g, rw_gn_b, rw_wo, at_wqkv, at_qn, at_kn, at_wo, dn_win, dn_conv, dn_alog, dn_dtb, dn_ng, dn_wo):
    bp, tp, d = x_prompt.shape
    bs, ts, _ = x_sample.shape
    assert tp == SEG and ts % SEG == 0 and bs + 1 <= SUBLANES
    parts = ts // SEG
    dims = (bp, bs, parts)
    depth = w_mod.shape[0]
    alpha = (2.0 * depth) ** 0.25

    x = jnp.concatenate([x_prompt.reshape(bp * tp, d), x_sample.reshape(bs * ts, d)], axis=0)
    cond = jnp.concatenate([c, c_ctx[None, :], jnp.zeros((SUBLANES - bs - 1, d), F32)], axis=0)
    mods = _modulation(cond, w_mod, b_mod)
    seg_row = jnp.array([bs] * bp + [b for b in range(bs) for _ in range(parts)], jnp.int32)
    modseg = [mods[l][seg_row].reshape(bp + bs * parts, N_MOD, d) for l in range(depth)]

    def mixer_in_dtype(l):
        return F32 if l % 3 == 0 else BF16

    new_rwkv, new_k, new_v, new_delta = [], [], [], []
    h = _premod(x, modseg[0], mixer_in_dtype(0))
    for l in range(depth):
        kind, j = l % 3, l // 3
        if kind == 0:
            z64 = jnp.zeros((RW_HEAD, d), F32)
            prm = dict(
                mu=rw_mu[j], wr=_bf(rw_wrkv[j, 0]), wk=_bf(rw_wrkv[j, 1]), wv=_bf(rw_wrkv[j, 2]),
                w1=_bf(jnp.concatenate([rw_w1[j, 0], rw_w1[j, 1]], axis=1)),
                w2=_bf(jnp.concatenate([jnp.concatenate([rw_w2[j, 0], z64], axis=1),
                                        jnp.concatenate([z64, rw_w2[j, 1]], axis=1)], axis=0)),
                a1=_bf(jnp.concatenate([rw_a1[j, 0], rw_a1[j, 1]], axis=1)),
                a2=_bf(jnp.concatenate([jnp.concatenate([rw_a2[j, 0], z64], axis=1),
                                        jnp.concatenate([z64, rw_a2[j, 1]], axis=1)], axis=0)),
                g1=_bf(rw_g1[j]), g2=_bf(rw_g2[j]), w0=rw_w0[j], a0=rw_a0[j],
                kk=rw_kk[j], ka=rw_ka[j], rk=rw_rk[j].reshape(-1), wo=_bf(rw_wo[j]))
            rows, s_new = _rwkv_mixer(h, state_rwkv[:, j], prm, dims)
            new_rwkv.append(s_new)
            out = dict(kern=_rw_out_ln_kernel, rows=rows, vecs=[rw_gn_g[j].reshape(1, d), rw_gn_b[j].reshape(1, d)])
        elif kind == 1:
            prm = dict(wqkv=_bf(at_wqkv[j]), qn=at_qn[j], kn=at_kn[j], wo=_bf(at_wo[j]))
            y, kp, vp = _attention_mixer(h, cache_k[:, j], cache_v[:, j], prm, dims)
            new_k.append(kp.reshape(bp, tp, -1, AT_HEAD))
            new_v.append(vp.reshape(bp, tp, -1, AT_HEAD))
            out = dict(kern=_mm_ln_kernel, rows=[y], vecs=[])
        else:
            prm = dict(win=_bf(dn_win[j, :, :4 * d]), wg=_bf(_gate_columns(dn_win[j, :, 4 * d:])),
                       conv=dn_conv[j], ga=_gate_lanes(dn_alog[j]), gb=_gate_lanes(dn_dtb[j]),
                       wo=_bf(dn_wo[j]))
            rows, s_new = _deltanet_mixer(h, state_delta[:, j], prm, dims)
            new_delta.append(s_new)
            out = dict(kern=functools.partial(_dn_out_ln_kernel, dv=d // DN_HEADS), rows=rows,
                       vecs=[jnp.tile(dn_ng[j], DN_HEADS).reshape(1, d)])
        x, h = _sublayer_ln(out["kern"], out["rows"], out["vecs"], [prm["wo"]], x, modseg[l], modseg[l],
                            ln_g[l, 0], ln_b[l, 0], alpha=alpha, gate_idx=2, sh_idx=3, h_dtype=BF16)
        nxt = min(l + 1, depth - 1)
        x, h = _sublayer_ln(_mlp_ln_kernel, [h], [], [_bf(w_fc1[l]), _bf(w_fc2[l])], x, modseg[l], modseg[nxt],
                            ln_g[l, 1], ln_b[l, 1], alpha=alpha, gate_idx=5, sh_idx=0,
                            h_dtype=mixer_in_dtype(l + 1) if l + 1 < depth else None, fc=d)
    y_prompt = x[:bp * tp].reshape(bp, tp, d)
    y_sample = x[bp * tp:].reshape(bs, ts, d)
    return (y_prompt, y_sample, jnp.stack(new_rwkv, axis=1), jnp.stack(new_k, axis=1),
            jnp.stack(new_v, axis=1), jnp.stack(new_delta, axis=1))
```
